```python
import math
import jax, jax.numpy as jnp
from jax import lax
import numpy as np

D_MODEL = 1024
BATCH = 2
SEQ = 8192
DEPTH = 1

MEM_LEN = 256
EPS = 1e-6
CONV_WIDTH = 4
SSD_EXPAND = 2
SSD_WIDTH = SSD_EXPAND * D_MODEL
SSD_HEAD_DIM = 64
SSD_HEADS = SSD_WIDTH // SSD_HEAD_DIM
SSD_GROUPS = 4
SSD_STATE = 128
SSD_CHUNK = 128
SSD_CONV_CH = SSD_WIDTH + 2 * SSD_GROUPS * SSD_STATE
LRU_WIDTH = 3 * D_MODEL // 2
LRU_BLOCKS = 16
LRU_BLOCK = LRU_WIDTH // LRU_BLOCKS
LRU_C = 8.0
MEM_HEADS = 4
MEM_HEAD_DIM = D_MODEL // MEM_HEADS
N_BRANCH = 3
SPLIT_POINTS = (
    SSD_WIDTH,
    SSD_WIDTH + SSD_CONV_CH,
    SSD_WIDTH + SSD_CONV_CH + SSD_HEADS,
    SSD_WIDTH + SSD_CONV_CH + SSD_HEADS + LRU_WIDTH,
    SSD_WIDTH + SSD_CONV_CH + SSD_HEADS + 2 * LRU_WIDTH,
    SSD_WIDTH + SSD_CONV_CH + SSD_HEADS + 2 * LRU_WIDTH + D_MODEL,
)
IN_WIDTH = SSD_WIDTH + SSD_CONV_CH + SSD_HEADS + 2 * LRU_WIDTH + D_MODEL + N_BRANCH * D_MODEL

kernel_name = "hybrid_ssd_rglru_memxattn_gated_block"


def rms_norm(x, g):
    xf = x.astype(jnp.float32)
    y = xf * lax.rsqrt(jnp.mean(xf * xf, axis=-1, keepdims=True) + EPS)
    return (y * g.astype(jnp.float32)).astype(x.dtype)


def causal_dwconv(x, w, b):
    k = w.shape[0]
    y = lax.conv_general_dilated(
        x, w[:, None, :].astype(x.dtype), window_strides=(1,), padding=[(k - 1, 0)],
        dimension_numbers=('NWC', 'WIO', 'NWC'), feature_group_count=x.shape[-1])
    return y + b


def ssd_scan(x, dt, a_neg, b_in, c_in):
    f32 = jnp.float32
    bsz, s, h, p = x.shape
    g, n = b_in.shape[2], b_in.shape[3]
    k = h // g
    l = SSD_CHUNK
    nc = s // l
    x = x.astype(f32).reshape(bsz, nc, l, g, k, p)
    dt = dt.astype(f32).reshape(bsz, nc, l, g, k)
    bm = b_in.astype(f32).reshape(bsz, nc, l, g, n)
    cm = c_in.astype(f32).reshape(bsz, nc, l, g, n)
    xdt = x * dt[..., None]
    a_cs = jnp.cumsum(dt * a_neg.astype(f32).reshape(g, k), axis=2)
    causal = jnp.tril(jnp.ones((l, l), dtype=bool))
    seg = a_cs[:, :, :, None] - a_cs[:, :, None, :]
    decay = jnp.exp(jnp.where(causal[:, :, None, None], seg, -jnp.inf))
    cb = jnp.einsum('bclgn,bcsgn->bclsg', cm, bm)
    y_diag = jnp.einsum('bclsgk,bcsgkp->bclgkp', decay * cb[..., None], xdt)
    decay_to_end = jnp.exp(a_cs[:, :, -1:] - a_cs)
    states = jnp.einsum('bclgn,bclgkp->bcgkpn', bm, xdt * decay_to_end[..., None])
    chunk_decay = jnp.exp(a_cs[:, :, -1])

    def step(carry, inp):
        st, dec = inp
        return carry * dec[..., None, None] + st, carry

    init = jnp.zeros((bsz, g, k, p, n), f32)
    _, prev = lax.scan(step, init, (jnp.moveaxis(states, 1, 0), jnp.moveaxis(chunk_decay, 1, 0)))
    prev = jnp.moveaxis(prev, 0, 1)
    y_off = jnp.einsum('bclgn,bcgkpn->bclgkp', cm, prev) * jnp.exp(a_cs)[..., None]
    return (y_diag + y_off).reshape(bsz, s, h, p)


def rg_lru(x, w_a, b_a, w_x, b_x, lam):
    f32 = jnp.float32
    bsz, s, w = x.shape
    xb = x.reshape(bsz, s, LRU_BLOCKS, LRU_BLOCK)
    r = jax.nn.sigmoid(jnp.einsum('bsni,nij->bsnj', xb, w_a) + b_a).reshape(bsz, s, w)
    i = jax.nn.sigmoid(jnp.einsum('bsni,nij->bsnj', xb, w_x) + b_x).reshape(bsz, s, w)
    log_a = (-LRU_C * r.astype(f32)) * jax.nn.softplus(-lam.astype(f32))
    a = jnp.exp(log_a)
    mult = jnp.sqrt(-jnp.expm1(2.0 * log_a))
    u = mult * (i * x).astype(f32)

    def combine(left, right):
        a1, b1 = left
        a2, b2 = right
        return a1 * a2, a2 * b1 + b2

    _, hs = lax.associative_scan(combine, (a, u), axis=1)
    return hs.astype(x.dtype)


def memory_attention(q, mem_n, w_kv):
    bsz, s, _ = q.shape
    m = mem_n.shape[1]
    kv = mem_n @ w_kv
    k, v = jnp.split(kv, 2, axis=-1)
    q = q.reshape(bsz, s, MEM_HEADS, MEM_HEAD_DIM)
    k = k.reshape(bsz, m, MEM_HEADS, MEM_HEAD_DIM)
    v = v.reshape(bsz, m, MEM_HEADS, MEM_HEAD_DIM)
    scores = jnp.einsum('bshd,bmhd->bhsm', q, k).astype(jnp.float32) * (MEM_HEAD_DIM ** -0.5)
    probs = jax.nn.softmax(scores, axis=-1).astype(v.dtype)
    return jnp.einsum('bhsm,bmhd->bshd', probs, v).reshape(bsz, s, D_MODEL)


def setup_inputs(seed: int = 0) -> dict:
    key = jax.random.key(seed)
    ks = jax.random.split(key, 24)
    f32 = jnp.float32
    nrm = lambda k, shape, scale: jax.random.normal(k, shape, f32) * scale
    x = jax.random.normal(ks[0], (BATCH, SEQ, D_MODEL), f32)
    mem = jax.random.normal(ks[1], (BATCH, MEM_LEN, D_MODEL), f32)
    norm_g = 1.0 + nrm(ks[2], (DEPTH, D_MODEL), 0.02)
    w_in = nrm(ks[3], (DEPTH, D_MODEL, IN_WIDTH), D_MODEL ** -0.5)
    ssd_conv_w = nrm(ks[4], (DEPTH, CONV_WIDTH, SSD_CONV_CH), CONV_WIDTH ** -0.5)
    ssd_conv_b = nrm(ks[5], (DEPTH, SSD_CONV_CH), 0.02)
    dt0 = jnp.exp(jax.random.uniform(ks[6], (DEPTH, SSD_HEADS), f32, math.log(1e-3), math.log(1e-1)))
    ssd_dt_bias = dt0 + jnp.log(-jnp.expm1(-dt0))
    ssd_a_log = jnp.log(jax.random.uniform(ks[7], (DEPTH, SSD_HEADS), f32, 1.0, 16.0))
    ssd_d = 1.0 + nrm(ks[8], (DEPTH, SSD_HEADS), 0.02)
    ssd_norm_g = 1.0 + nrm(ks[9], (DEPTH, SSD_GROUPS, SSD_WIDTH // SSD_GROUPS), 0.02)
    lru_conv_w = nrm(ks[10], (DEPTH, CONV_WIDTH, LRU_WIDTH), CONV_WIDTH ** -0.5)
    lru_conv_b = nrm(ks[11], (DEPTH, LRU_WIDTH), 0.02)
    lru_w_a = nrm(ks[12], (DEPTH, LRU_BLOCKS, LRU_BLOCK, LRU_BLOCK), LRU_BLOCK ** -0.5)
    lru_b_a = nrm(ks[13], (DEPTH, LRU_BLOCKS, LRU_BLOCK), 0.02)
    lru_w_x = nrm(ks[14], (DEPTH, LRU_BLOCKS, LRU_BLOCK, LRU_BLOCK), LRU_BLOCK ** -0.5)
    lru_b_x = nrm(ks[15], (DEPTH, LRU_BLOCKS, LRU_BLOCK), 0.02)
    a8 = jax.random.uniform(ks[16], (DEPTH, LRU_WIDTH), f32, 0.9, 0.999)
    sig = a8 ** (1.0 / LRU_C)
    lru_lambda = jnp.log(sig) - jnp.log1p(-sig)
    mem_norm_g = 1.0 + nrm(ks[17], (DEPTH, D_MODEL), 0.02)
    w_kv = nrm(ks[18], (DEPTH, D_MODEL, 2 * D_MODEL), D_MODEL ** -0.5)
    w_br_ssd = nrm(ks[19], (DEPTH, SSD_WIDTH, D_MODEL), SSD_WIDTH ** -0.5)
    w_br_lru = nrm(ks[20], (DEPTH, LRU_WIDTH, D_MODEL), LRU_WIDTH ** -0.5)
    w_br_mem = nrm(ks[21], (DEPTH, D_MODEL, D_MODEL), D_MODEL ** -0.5)
    w_out = nrm(ks[22], (DEPTH, D_MODEL, D_MODEL), D_MODEL ** -0.5)
    final_g = 1.0 + nrm(ks[23], (D_MODEL,), 0.02)
    return {"x": x, "mem": mem, "norm_g": norm_g, "w_in": w_in,
            "ssd_conv_w": ssd_conv_w, "ssd_conv_b": ssd_conv_b, "ssd_dt_bias": ssd_dt_bias,
            "ssd_a_log": ssd_a_log, "ssd_d": ssd_d, "ssd_norm_g": ssd_norm_g,
            "lru_conv_w": lru_conv_w, "lru_conv_b": lru_conv_b, "lru_w_a": lru_w_a,
            "lru_b_a": lru_b_a, "lru_w_x": lru_w_x, "lru_b_x": lru_b_x, "lru_lambda": lru_lambda,
            "mem_norm_g": mem_norm_g, "w_kv": w_kv, "w_br_ssd": w_br_ssd, "w_br_lru": w_br_lru,
            "w_br_mem": w_br_mem, "w_out": w_out, "final_g": final_g}


def reference(x, mem, norm_g, w_in, ssd_conv_w, ssd_conv_b, ssd_dt_bias, ssd_a_log, ssd_d,
              ssd_norm_g, lru_conv_w, lru_conv_b, lru_w_a, lru_b_a, lru_w_x, lru_b_x, lru_lambda,
              mem_norm_g, w_kv, w_br_ssd, w_br_lru, w_br_mem, w_out, final_g):
    bsz, s, _ = x.shape
    for l in range(DEPTH):
        h = rms_norm(x, norm_g[l])
        proj = h @ w_in[l]
        z, xbc, dt_raw, lru_gate, lru_x, q, gate_logits = jnp.split(proj, SPLIT_POINTS, axis=-1)

        xbc = jax.nn.silu(causal_dwconv(xbc, ssd_conv_w[l], ssd_conv_b[l]))
        xs, bs, cs = jnp.split(xbc, [SSD_WIDTH, SSD_WIDTH + SSD_GROUPS * SSD_STATE], axis=-1)
        xs = xs.reshape(bsz, s, SSD_HEADS, SSD_HEAD_DIM)
        dt = jax.nn.softplus((dt_raw + ssd_dt_bias[l]).astype(jnp.float32))
        y = ssd_scan(xs, dt, -jnp.exp(ssd_a_log[l].astype(jnp.float32)),
                     bs.reshape(bsz, s, SSD_GROUPS, SSD_STATE), cs.reshape(bsz, s, SSD_GROUPS, SSD_STATE))
        y = (y + xs.astype(jnp.float32) * ssd_d[l][:, None].astype(jnp.float32)).astype(x.dtype)
        y = y.reshape(bsz, s, SSD_WIDTH) * jax.nn.silu(z)
        y_ssd = rms_norm(y.reshape(bsz, s, SSD_GROUPS, SSD_WIDTH // SSD_GROUPS),
                         ssd_norm_g[l]).reshape(bsz, s, SSD_WIDTH)

        xl = causal_dwconv(lru_x, lru_conv_w[l], lru_conv_b[l])
        y_lru = rg_lru(xl, lru_w_a[l], lru_b_a[l], lru_w_x[l], lru_b_x[l], lru_lambda[l]) * jax.nn.silu(lru_gate)

        mem_n = rms_norm(mem, mem_norm_g[l])
        y_mem = memory_attention(q, mem_n, w_kv[l])

        gates = jax.nn.sigmoid(gate_logits).reshape(bsz, s, N_BRANCH, D_MODEL)
        merged = (gates[:, :, 0] * (y_ssd @ w_br_ssd[l])
                  + gates[:, :, 1] * (y_lru @ w_br_lru[l])
                  + gates[:, :, 2] * (y_mem @ w_br_mem[l]))
        x = x + merged @ w_out[l]
    return rms_norm(x, final_g)
```

```python
import functools

import jax
import jax.numpy as jnp
from jax import lax
from jax.experimental import pallas as pl
from jax.experimental.pallas import tpu as pltpu

F32 = jnp.float32
BF16 = jnp.bfloat16

EPS = 1e-6
CONV_WIDTH = 4
SSD_HEAD_DIM = 64
SSD_GROUPS = 4
SSD_STATE = 128
SSD_CHUNK = 128
LRU_BLOCKS = 16
LRU_C = 8.0
MEM_HEADS = 4
LANES = 128
SUBLANES = 8
LRU_TILE_GROUPS = 4

VMEM_LIMIT = 56 * 1024 * 1024


def _cparams(sem):
    return pltpu.CompilerParams(dimension_semantics=sem, vmem_limit_bytes=VMEM_LIMIT)


def _sigmoid(x):
    return 1.0 / (1.0 + jnp.exp(-x))


def _silu(x):
    return x * _sigmoid(x)


def _softplus(x):
    return jnp.maximum(x, 0.0) + jnp.log(1.0 + jnp.exp(-jnp.abs(x)))


def _split2(x):
    hi = x.astype(BF16)
    lo = (x - hi.astype(F32)).astype(BF16)
    return hi, lo


def _split3(x):
    hi = x.astype(BF16)
    r = x - hi.astype(F32)
    mid = r.astype(BF16)
    lo = (r - mid.astype(F32)).astype(BF16)
    return hi, mid, lo


def _dot(a, b):
    return jnp.dot(a, b, preferred_element_type=F32)


def _dot_nt(a, b):
    return lax.dot_general(a, b, (((1,), (1,)), ((), ())), preferred_element_type=F32)


def _rmsnorm_kernel(x_ref, g_ref, o_ref):
    x = x_ref[...]
    ms = jnp.mean(x * x, axis=-1, keepdims=True)
    o_ref[...] = (x * lax.rsqrt(ms + EPS) * g_ref[...]).astype(o_ref.dtype)


def _rmsnorm_bf16(x2d, g, tm):
    t, d = x2d.shape
    return pl.pallas_call(
        _rmsnorm_kernel,
        grid=(t // tm,),
        in_specs=[pl.BlockSpec((tm, d), lambda i: (i, 0)),
                  pl.BlockSpec((1, d), lambda i: (0, 0))],
        out_specs=pl.BlockSpec((tm, d), lambda i: (i, 0)),
        out_shape=jax.ShapeDtypeStruct((t, d), BF16),
        compiler_params=_cparams(("parallel",)),
        name="rmsnorm",
    )(x2d, g.reshape(1, d))


def _matmul_kernel(h_ref, w_ref, o_ref):
    o_ref[...] = _dot(h_ref[...], w_ref[...]).astype(o_ref.dtype)


def _matmul(h, w, out_dtype, tm, tn, name):
    t, k = h.shape
    n = w.shape[1]
    return pl.pallas_call(
        _matmul_kernel,
        grid=(n // tn, t // tm),
        in_specs=[pl.BlockSpec((tm, k), lambda j, i: (i, 0)),
                  pl.BlockSpec((k, tn), lambda j, i: (0, j))],
        out_specs=pl.BlockSpec((tm, tn), lambda j, i: (i, j)),
        out_shape=jax.ShapeDtypeStruct((t, n), out_dtype),
        compiler_params=_cparams(("parallel", "parallel")),
        name=name,
    )(h, w)


def _causal_conv(xpad_ref, x_new, w_ref, b_ref, first):
    rows = x_new.shape[0]

    @pl.when(first)
    def _():
        xpad_ref[0:SUBLANES, :] = jnp.zeros((SUBLANES, xpad_ref.shape[1]), F32)

    xpad_ref[SUBLANES:SUBLANES + rows, :] = x_new
    acc = b_ref[...] + w_ref[CONV_WIDTH - 1:CONV_WIDTH, :] * x_new
    for k in range(CONV_WIDTH - 1):
        off = SUBLANES - (CONV_WIDTH - 1) + k
        acc = acc + w_ref[k:k + 1, :] * xpad_ref[off:off + rows, :]
    xpad_ref[0:SUBLANES, :] = xpad_ref[rows:rows + SUBLANES, :]
    return acc


def _ssd_kernel(z_ref, xbc_ref, dt_ref, cw_ref, cb_ref, dtb_ref, alog_ref, dexp_ref, ng_ref, e_ref,
                o_ref, xpad_ref, state_ref, *, width, groups, nstate):
    c = pl.program_id(1)
    L = z_ref.shape[0]
    gw = width // groups
    pair = LANES
    pairs_per_group = gw // pair

    @pl.when(c == 0)
    def _():
        state_ref[...] = jnp.zeros(state_ref.shape, F32)

    xbc = _silu(_causal_conv(xpad_ref, xbc_ref[...].astype(F32), cw_ref, cb_ref, c == 0))
    xs = xbc[:, :width]
    bmat = xbc[:, width:width + groups * nstate]
    cmat = xbc[:, width + groups * nstate:]

    dt = _softplus(dt_ref[...] + dtb_ref[...])
    da = dt * (-jnp.exp(alog_ref[...]))
    row = lax.broadcasted_iota(jnp.int32, (L, L), 0)
    col = lax.broadcasted_iota(jnp.int32, (L, L), 1)
    causal = row >= col
    tril = causal.astype(BF16)
    a_cs = sum(_dot(tril, part) for part in _split3(da))
    a_cs_t = a_cs.T
    a_end = a_cs[L - 1:L, :]
    ea = jnp.exp(a_cs)
    wgt = dt * jnp.exp(a_end - a_cs)

    e = e_ref[...]

    def expand(v):
        hi, lo = _split2(v)
        return _dot(hi, e) + _dot(lo, e)

    dt_x = expand(dt)
    ea_x = expand(ea)
    wgt_x = expand(wgt)
    xdt = xs * dt_x
    xw = (xs * wgt_x).astype(BF16)
    lane = lax.broadcasted_iota(jnp.int32, (L, pair), 1)
    lo_half = lane < SSD_HEAD_DIM

    ys = []
    for g in range(groups):
        cg = cmat[:, g * nstate:(g + 1) * nstate].astype(BF16)
        bg = bmat[:, g * nstate:(g + 1) * nstate]
        cb = _dot_nt(cg, bg.astype(BF16))
        s_old = state_ref[g]
        y_off = _dot(cg, s_old.astype(BF16)) * ea_x[:, g * gw:(g + 1) * gw]
        y_diag = []
        for j in range(pairs_per_group):
            h0 = (g * pairs_per_group + j) * 2
            ms = []
            for h in (h0, h0 + 1):
                seg = a_cs[:, h:h + 1] - a_cs_t[h:h + 1, :]
                ms.append(jnp.exp(jnp.where(causal, seg, -jnp.inf)) * cb)
            mcat = jnp.concatenate(ms, axis=1).astype(BF16)
            xp = xdt[:, h0 * SSD_HEAD_DIM:(h0 + 2) * SSD_HEAD_DIM]
            rhs = jnp.concatenate([jnp.where(lo_half, xp, 0.0), jnp.where(lo_half, 0.0, xp)],
                                  axis=0).astype(BF16)
            y_diag.append(_dot(mcat, rhs))
        ys.append(jnp.concatenate(y_diag, axis=1) + y_off)
        s_new = s_old * ea_x[L - 1:L, g * gw:(g + 1) * gw] + _dot(bg.T.astype(BF16), xw[:, g * gw:(g + 1) * gw])
        state_ref[g] = s_new

    y = jnp.concatenate(ys, axis=1) + xs * dexp_ref[...]
    y = y * _silu(z_ref[...].astype(F32))
    outs = []
    for g in range(groups):
        yg = y[:, g * gw:(g + 1) * gw]
        ms = jnp.mean(yg * yg, axis=-1, keepdims=True)
        outs.append(yg * lax.rsqrt(ms + EPS))
    o_ref[...] = (jnp.concatenate(outs, axis=1) * ng_ref[...]).astype(o_ref.dtype)


def _ssd_branch(z, xbc, dt_raw, conv_w, conv_b, dt_bias, a_log, d_skip, norm_g, bsz, seq):
    t, width = z.shape
    conv_ch = xbc.shape[1]
    heads = dt_bias.shape[0]
    groups = SSD_GROUPS
    nstate = SSD_STATE
    L = SSD_CHUNK
    nc = seq // L
    pad = LANES - heads
    dtb = jnp.pad(dt_bias, (0, pad)).reshape(1, LANES)
    alog = jnp.pad(a_log, (0, pad)).reshape(1, LANES)
    dexp = jnp.repeat(d_skip, SSD_HEAD_DIM).reshape(1, width)
    expand = (jnp.arange(LANES)[:, None] == (jnp.arange(width) // SSD_HEAD_DIM)[None, :]).astype(BF16)
    row = lambda b, c: (b * nc + c, 0)
    const = lambda b, c: (0, 0)
    return pl.pallas_call(
        functools.partial(_ssd_kernel, width=width, groups=groups, nstate=nstate),
        grid=(bsz, nc),
        in_specs=[pl.BlockSpec((L, width), row),
                  pl.BlockSpec((L, conv_ch), row),
                  pl.BlockSpec((L, LANES), row),
                  pl.BlockSpec((CONV_WIDTH, conv_ch), const),
                  pl.BlockSpec((1, conv_ch), const),
                  pl.BlockSpec((1, LANES), const),
                  pl.BlockSpec((1, LANES), const),
                  pl.BlockSpec((1, width), const),
                  pl.BlockSpec((1, width), const),
                  pl.BlockSpec((LANES, width), const)],
        out_specs=pl.BlockSpec((L, width), row),
        out_shape=jax.ShapeDtypeStruct((t, width), BF16),
        scratch_shapes=[pltpu.VMEM((L + SUBLANES, conv_ch), F32),
                        pltpu.VMEM((groups, nstate, width // groups), F32)],
        compiler_params=_cparams(("arbitrary", "arbitrary")),
        name="ssd",
    )(z, xbc, dt_raw, conv_w, conv_b.reshape(1, conv_ch), dtb, alog, dexp, norm_g.reshape(1, width), expand)


def _lru_kernel(lg_ref, lx_ref, cw_ref, cb_ref, wg_ref, ba_ref, bx_ref, lam_ref,
                o_ref, xpad_ref, a_ref, u_ref, h_ref):
    c = pl.program_id(1)
    rows, width = lg_ref.shape
    tg = width // LRU_TILE_GROUPS

    @pl.when(c == 0)
    def _():
        h_ref[...] = jnp.zeros(h_ref.shape, F32)

    xl = _causal_conv(xpad_ref, lx_ref[...].astype(F32), cw_ref, cb_ref, c == 0)
    xl_b = xl.astype(BF16)
    ga, gx = [], []
    for g in range(LRU_TILE_GROUPS):
        gg = _dot(xl_b[:, g * tg:(g + 1) * tg], wg_ref[g])
        ga.append(gg[:, :tg])
        gx.append(gg[:, tg:])
    r = _sigmoid(jnp.concatenate(ga, axis=1) + ba_ref[...])
    i = _sigmoid(jnp.concatenate(gx, axis=1) + bx_ref[...])
    log_a = (-LRU_C * r) * _softplus(-lam_ref[...])
    a = jnp.exp(log_a)
    mult = jnp.sqrt(1.0 - jnp.exp(2.0 * log_a))
    a_ref[...] = a
    u_ref[...] = mult * (i * xl)

    sub = lax.broadcasted_iota(jnp.int32, (SUBLANES, width), 0)

    def body(blk, h):
        start = pl.multiple_of(blk * SUBLANES, SUBLANES)
        a_t = a_ref[pl.ds(start, SUBLANES), :]
        u_t = u_ref[pl.ds(start, SUBLANES), :]
        out = jnp.zeros((SUBLANES, width), F32)
        for s in range(SUBLANES):
            cand = a_t * h + u_t
            h = jnp.broadcast_to(cand[s:s + 1, :], (SUBLANES, width))
            out = jnp.where(sub == s, cand, out)
        u_ref[pl.ds(start, SUBLANES), :] = out
        return h

    h_last = lax.fori_loop(0, rows // SUBLANES, body, h_ref[...])
    h_ref[...] = h_last
    o_ref[...] = (u_ref[...] * _silu(lg_ref[...].astype(F32))).astype(o_ref.dtype)


def _lru_branch(lg, lx, conv_w, conv_b, w_a, b_a, w_x, b_x, lam, bsz, seq, rows):
    t, width = lg.shape
    nb = seq // rows
    nblk, blk = w_a.shape[0], w_a.shape[1]
    per = nblk // LRU_TILE_GROUPS
    tg = width // LRU_TILE_GROUPS

    def tile_diag(w):
        wt = w.reshape(LRU_TILE_GROUPS, per, blk, blk)
        eye = jnp.eye(per, dtype=w.dtype)
        return jnp.einsum("gpij,pq->gpiqj", wt, eye).reshape(LRU_TILE_GROUPS, tg, tg)

    wg = jnp.concatenate([tile_diag(w_a), tile_diag(w_x)], axis=2).astype(BF16)
    row = lambda b, c: (b * nb + c, 0)
    const = lambda b, c: (0, 0)
    return pl.pallas_call(
        _lru_kernel,
        grid=(bsz, nb),
        in_specs=[pl.BlockSpec((rows, width), row),
                  pl.BlockSpec((rows, width), row),
                  pl.BlockSpec((CONV_WIDTH, width), const),
                  pl.BlockSpec((1, width), const),
                  pl.BlockSpec((LRU_TILE_GROUPS, tg, 2 * tg), lambda b, c: (0, 0, 0)),
                  pl.BlockSpec((1, width), const),
                  pl.BlockSpec((1, width), const),
                  pl.BlockSpec((1, width), const)],
        out_specs=pl.BlockSpec((rows, width), row),
        out_shape=jax.ShapeDtypeStruct((t, width), BF16),
        scratch_shapes=[pltpu.VMEM((rows + SUBLANES, width), F32),
                        pltpu.VMEM((rows, width), F32),
                        pltpu.VMEM((rows, width), F32),
                        pltpu.VMEM((SUBLANES, width), F32)],
        compiler_params=_cparams(("arbitrary", "arbitrary")),
        name="lru",
    )(lg, lx, conv_w, conv_b.reshape(1, width), wg, b_a.reshape(1, width), b_x.reshape(1, width),
      lam.reshape(1, width))


def _kv_kernel(m_ref, g_ref, w_ref, o_ref):
    m = m_ref[...]
    ms = jnp.mean(m * m, axis=-1, keepdims=True)
    mn = (m * lax.rsqrt(ms + EPS) * g_ref[...]).astype(BF16)
    o_ref[...] = _dot(mn, w_ref[...]).astype(o_ref.dtype)


def _mem_kv(mem2d, g, w_kv_b, tn):
    m, d = mem2d.shape
    n = w_kv_b.shape[1]
    return pl.pallas_call(
        _kv_kernel,
        grid=(n // tn,),
        in_specs=[pl.BlockSpec((m, d), lambda j: (0, 0)),
                  pl.BlockSpec((1, d), lambda j: (0, 0)),
                  pl.BlockSpec((d, tn), lambda j: (0, j))],
        out_specs=pl.BlockSpec((m, tn), lambda j: (0, j)),
        out_shape=jax.ShapeDtypeStruct((m, n), BF16),
        compiler_params=_cparams(("parallel",)),
        name="mem_kv",
    )(mem2d, g.reshape(1, d), w_kv_b)


def _attn_kernel(q_ref, kv_ref, o_ref):
    d = q_ref.shape[1]
    hd = d // MEM_HEADS
    scale = hd ** -0.5
    outs = []
    for h in range(MEM_HEADS):
        qh = q_ref[:, h * hd:(h + 1) * hd]
        kh = kv_ref[:, h * hd:(h + 1) * hd]
        vh = kv_ref[:, d + h * hd:d + (h + 1) * hd]
        s = _dot_nt(qh, kh) * scale
        p = jnp.exp(s - jnp.max(s, axis=-1, keepdims=True))
        l = jnp.sum(p, axis=-1, keepdims=True)
        outs.append(_dot(p.astype(BF16), vh) / l)
    o_ref[...] = jnp.concatenate(outs, axis=1).astype(o_ref.dtype)


def _mem_attention(q, kv, bsz, seq, tq):
    t, d = q.shape
    m = kv.shape[0] // bsz
    nq = seq // tq
    return pl.pallas_call(
        _attn_kernel,
        grid=(bsz, nq),
        in_specs=[pl.BlockSpec((tq, d), lambda b, i: (b * nq + i, 0)),
                  pl.BlockSpec((m, 2 * d), lambda b, i: (b, 0))],
        out_specs=pl.BlockSpec((tq, d), lambda b, i: (b * nq + i, 0)),
        out_shape=jax.ShapeDtypeStruct((t, d), BF16),
        compiler_params=_cparams(("parallel", "parallel")),
        name="mem_attn",
    )(q, kv)


def _merge_kernel(ys_ref, yl_ref, ym_ref, gl_ref, x_ref, ws_ref, wl_ref, wm_ref, wo_ref, fg_ref, o_ref, *,
                  final_norm):
    d = x_ref.shape[1]
    gates = _sigmoid(gl_ref[...].astype(F32))
    merged = (gates[:, :d] * _dot(ys_ref[...], ws_ref[...])
              + gates[:, d:2 * d] * _dot(yl_ref[...], wl_ref[...])
              + gates[:, 2 * d:] * _dot(ym_ref[...], wm_ref[...]))
    xn = x_ref[...] + _dot(merged.astype(BF16), wo_ref[...])
    if final_norm:
        ms = jnp.mean(xn * xn, axis=-1, keepdims=True)
        xn = xn * lax.rsqrt(ms + EPS) * fg_ref[...]
    o_ref[...] = xn


def _merge(y_ssd, y_lru, y_mem, gate_logits, x2d, w_s, w_l, w_m, w_o, final_g, final_norm, tm):
    t, d = x2d.shape
    row = lambda i: (i, 0)
    const = lambda i: (0, 0)
    full = lambda a: pl.BlockSpec(a.shape, const)
    return pl.pallas_call(
        functools.partial(_merge_kernel, final_norm=final_norm),
        grid=(t // tm,),
        in_specs=[pl.BlockSpec((tm, y_ssd.shape[1]), row),
                  pl.BlockSpec((tm, y_lru.shape[1]), row),
                  pl.BlockSpec((tm, y_mem.shape[1]), row),
                  pl.BlockSpec((tm, gate_logits.shape[1]), row),
                  pl.BlockSpec((tm, d), row),
                  full(w_s), full(w_l), full(w_m), full(w_o),
                  pl.BlockSpec((1, d), const)],
        out_specs=pl.BlockSpec((tm, d), row),
        out_shape=jax.ShapeDtypeStruct((t, d), F32),
        compiler_params=_cparams(("parallel",)),
        name="merge",
    )(y_ssd, y_lru, y_mem, gate_logits, x2d, w_s, w_l, w_m, w_o, final_g.reshape(1, d))


def _pick(n, prefs):
    for p in prefs:
        if n % p == 0:
            return p
    return n


def kernel(x, mem, norm_g, w_in, ssd_conv_w, ssd_conv_b, ssd_dt_bias, ssd_a_log, ssd_d, ssd_norm_g, lru_conv_w, lru_conv_b, lru_w_a, lru_b_a, lru_w_x, lru_b_x, lru_lambda, mem_norm_g, w_kv, w_br_ssd, w_br_lru, w_br_mem, w_out, final_g):
    bsz, seq, d = x.shape
    depth = norm_g.shape[0]
    t = bsz * seq
    heads = ssd_dt_bias.shape[1]
    ssd_w = heads * SSD_HEAD_DIM
    conv_ch = ssd_conv_w.shape[2]
    lru_w = lru_conv_w.shape[2]
    tm = _pick(t, (512, 256, 128))
    x2d = x.reshape(t, d)
    mem2d = mem.reshape(bsz * mem.shape[1], d)
    for l in range(depth):
        h = _rmsnorm_bf16(x2d, norm_g[l], tm)
        w = w_in[l]
        o = 0

        def section(n):
            nonlocal o
            ws = w[:, o:o + n]
            o += n
            return ws

        w_z, w_xbc, w_dt = section(ssd_w), section(conv_ch), section(heads)
        w_lg, w_lx, w_q, w_g = section(lru_w), section(lru_w), section(d), section(3 * d)
        w_dt = jnp.pad(w_dt, ((0, 0), (0, LANES - heads)))
        z = _matmul(h, w_z.astype(BF16), BF16, tm, _pick(ssd_w, (1024, 512, 256, 128)), "proj_z")
        xbc = _matmul(h, w_xbc.astype(BF16), BF16, tm, _pick(conv_ch, (1024, 512, 256, 128)), "proj_xbc")
        dt_raw = _matmul(h, w_dt.astype(BF16), F32, tm, LANES, "proj_dt")
        lg = _matmul(h, w_lg.astype(BF16), BF16, tm, _pick(lru_w, (768, 512, 256, 128)), "proj_lg")
        lx = _matmul(h, w_lx.astype(BF16), BF16, tm, _pick(lru_w, (768, 512, 256, 128)), "proj_lx")
        q = _matmul(h, w_q.astype(BF16), BF16, tm, _pick(d, (1024, 512, 256, 128)), "proj_q")
        gl = _matmul(h, w_g.astype(BF16), BF16, tm, _pick(3 * d, (1024, 512, 256, 128)), "proj_gates")

        y_ssd = _ssd_branch(z, xbc, dt_raw, ssd_conv_w[l], ssd_conv_b[l], ssd_dt_bias[l], ssd_a_log[l],
                            ssd_d[l], ssd_norm_g[l].reshape(-1), bsz, seq)
        y_lru = _lru_branch(lg, lx, lru_conv_w[l], lru_conv_b[l], lru_w_a[l], lru_b_a[l].reshape(-1),
                            lru_w_x[l], lru_b_x[l].reshape(-1), lru_lambda[l], bsz, seq,
                            _pick(seq, (256, 128)))
        kv = _mem_kv(mem2d, mem_norm_g[l], w_kv[l].astype(BF16), _pick(2 * d, (512, 256, 128)))
        y_mem = _mem_attention(q, kv, bsz, seq, _pick(seq, (512, 256, 128)))
        x2d = _merge(y_ssd, y_lru, y_mem, gl, x2d, w_br_ssd[l].astype(BF16), w_br_lru[l].astype(BF16),
                     w_br_mem[l].astype(BF16), w_out[l].astype(BF16), final_g, l == depth - 1, tm)
    return x2d.reshape(bsz, seq, d)
```

```python
import functools

import jax
import jax.numpy as jnp
from jax import lax
from jax.experimental import pallas as pl
from jax.experimental.pallas import tpu as pltpu

F32 = jnp.float32
BF16 = jnp.bfloat16

EPS = 1e-6
CONV_WIDTH = 4
SSD_HEAD_DIM = 64
SSD_GROUPS = 4
SSD_STATE = 128
SSD_CHUNK = 128
LRU_C = 8.0
MEM_HEADS = 4
LANES = 128
SUBLANES = 8
LRU_TILE_GROUPS = 4
SCAN_STRIDE = 4
SCAN_ROWS = SCAN_STRIDE * SUBLANES

VMEM_LIMIT = 56 * 1024 * 1024


def _cparams(sem):
    return pltpu.CompilerParams(dimension_semantics=sem, vmem_limit_bytes=VMEM_LIMIT)


LOG2E = 1.4426950408889634
TINY = 1e-30


def _sigmoid(x):
    return 1.0 / (1.0 + jnp.exp2(x * (-LOG2E)))


def _sqrt_nonneg(x):
    return x * lax.rsqrt(jnp.maximum(x, TINY))


def _silu(x):
    return x * _sigmoid(x)


def _softplus(x):
    return jnp.maximum(x, 0.0) + jnp.log1p(jnp.exp(-jnp.abs(x)))


def _split2(x):
    hi = x.astype(BF16)
    lo = (x - hi.astype(F32)).astype(BF16)
    return hi, lo


def _split3(x):
    hi = x.astype(BF16)
    r = x - hi.astype(F32)
    mid = r.astype(BF16)
    lo = (r - mid.astype(F32)).astype(BF16)
    return hi, mid, lo


def _dot(a, b):
    return jnp.dot(a, b, preferred_element_type=F32)


def _dot_nt(a, b):
    return lax.dot_general(a, b, (((1,), (1,)), ((), ())), preferred_element_type=F32)


def _pick(n, prefs):
    for p in prefs:
        if n % p == 0:
            return p
    return n


def _rmsnorm_kernel(x_ref, g_ref, o_ref):
    x = x_ref[...]
    ms = jnp.mean(x * x, axis=-1, keepdims=True)
    o_ref[...] = (x * lax.rsqrt(ms + EPS) * g_ref[...]).astype(o_ref.dtype)


def _rmsnorm_bf16(x2d, g, tm):
    t, d = x2d.shape
    return pl.pallas_call(
        _rmsnorm_kernel,
        grid=(t // tm,),
        in_specs=[pl.BlockSpec((tm, d), lambda i: (i, 0)),
                  pl.BlockSpec((1, d), lambda i: (0, 0))],
        out_specs=pl.BlockSpec((tm, d), lambda i: (i, 0)),
        out_shape=jax.ShapeDtypeStruct((t, d), BF16),
        compiler_params=_cparams(("parallel",)),
        name="rmsnorm",
    )(x2d, g.reshape(1, d))


def _proj_kernel(*refs, sub, act, conv, blocks_per_seq):
    if conv:
        h_ref, w_ref, cw_ref, cb_ref, o_ref, pad_ref = refs
    else:
        h_ref, w_ref, o_ref = refs
    tm = h_ref.shape[0]
    i = pl.program_id(1)

    if conv:
        @pl.when(i % blocks_per_seq == 0)
        def _():
            pad_ref[:, 0:2 * SUBLANES, :] = jnp.zeros((pad_ref.shape[0], 2 * SUBLANES, LANES), F32)

    for r in range(tm // sub):
        rows = slice(r * sub, (r + 1) * sub)
        acc = _dot(h_ref[rows, :], w_ref[...])
        if not conv:
            o_ref[rows, :] = (_silu(acc) if act == "silu" else _sigmoid(acc)).astype(o_ref.dtype)
            continue
        for c in range(pad_ref.shape[0]):
            cols = slice(c * LANES, (c + 1) * LANES)
            x = acc[:, cols]
            pad_ref[c, pl.ds(2 * SUBLANES, sub, stride=2), :] = x
            y = cb_ref[:, cols] + cw_ref[CONV_WIDTH - 1:CONV_WIDTH, cols] * x
            for s in range(1, CONV_WIDTH):
                y = y + (cw_ref[CONV_WIDTH - 1 - s:CONV_WIDTH - s, cols]
                         * pad_ref[c, pl.ds(2 * (SUBLANES - s), sub, stride=2), :])
            pad_ref[c, pl.ds(0, SUBLANES, stride=2), :] = x[sub - SUBLANES:, :]
            o_ref[rows, cols] = (_silu(y) if act == "silu" else y).astype(o_ref.dtype)


def _proj(h, w, col0, ncols, tn, seq, name, act=None, conv_w=None, conv_b=None, out_dtype=BF16):
    t, k = h.shape
    tm = _pick(seq, (2048, 1024, 512, 256, 128))
    sub = min(tm, 512)
    conv = conv_w is not None
    joff = col0 // tn
    in_specs = [pl.BlockSpec((tm, k), lambda j, i: (i, 0)),
                pl.BlockSpec((k, tn), lambda j, i: (0, j + joff))]
    args = [h, w]
    scratch = []
    if conv:
        in_specs += [pl.BlockSpec((CONV_WIDTH, tn), lambda j, i: (0, j)),
                     pl.BlockSpec((1, tn), lambda j, i: (0, j))]
        args += [conv_w, conv_b.reshape(1, ncols)]
        scratch = [pltpu.VMEM((tn // LANES, 2 * (sub + SUBLANES), LANES), F32)]
    return pl.pallas_call(
        functools.partial(_proj_kernel, sub=sub, act=act, conv=conv, blocks_per_seq=seq // tm),
        grid=(ncols // tn, t // tm),
        in_specs=in_specs,
        out_specs=pl.BlockSpec((tm, tn), lambda j, i: (i, j)),
        out_shape=jax.ShapeDtypeStruct((t, ncols), out_dtype),
        scratch_shapes=scratch,
        compiler_params=_cparams(("arbitrary", "arbitrary")),
        name=name,
    )(*args)


def _proj_q_dt_kernel(h_ref, wq_ref, wdt_ref, q_ref, dt_ref, *, sub):
    tm = h_ref.shape[0]
    for r in range(tm // sub):
        h = h_ref[r * sub:(r + 1) * sub, :]
        q_ref[r * sub:(r + 1) * sub, :] = _dot(h, wq_ref[...]).astype(q_ref.dtype)
        dt_ref[r * sub:(r + 1) * sub, :] = _dot(h, wdt_ref[...])


def _proj_q_dt(h, w, col0, ncols, w_dt, seq):
    t, k = h.shape
    tm = _pick(seq, (2048, 1024, 512, 256, 128))
    sub = min(tm, 512)
    joff = col0 // ncols
    return pl.pallas_call(
        functools.partial(_proj_q_dt_kernel, sub=sub),
        grid=(t // tm,),
        in_specs=[pl.BlockSpec((tm, k), lambda i: (i, 0)),
                  pl.BlockSpec((k, ncols), lambda i: (0, joff)),
                  pl.BlockSpec((k, LANES), lambda i: (0, 0))],
        out_specs=[pl.BlockSpec((tm, ncols), lambda i: (i, 0)),
                   pl.BlockSpec((tm, LANES), lambda i: (i, 0))],
        out_shape=[jax.ShapeDtypeStruct((t, ncols), BF16),
                   jax.ShapeDtypeStruct((t, LANES), F32)],
        compiler_params=_cparams(("arbitrary",)),
        name="proj_q_dt",
    )(h, w, w_dt)


def _ssd_kernel(zs_ref, xbc_ref, dt_ref, dtb_ref, alog_ref, dexp_ref, ng_ref, e_ref,
                o_ref, state_ref, yacc_ref, *, width, groups, nstate, heads):
    c = pl.program_id(1)
    L = SSD_CHUNK
    gw = width // groups
    pairs = gw // LANES
    b_off = width
    c_off = width + groups * nstate

    @pl.when(c == 0)
    def _():
        state_ref[...] = jnp.zeros(state_ref.shape, F32)

    row = lax.broadcasted_iota(jnp.int32, (L, L), 0)
    col = lax.broadcasted_iota(jnp.int32, (L, L), 1)
    causal = row >= col
    triu = (row <= col).astype(BF16)
    lo_half = lax.broadcasted_iota(jnp.int32, (L, LANES), 1) < SSD_HEAD_DIM

    def chunk(ci, carry):
        rows = pl.ds(pl.multiple_of(ci * L, L), L)
        dt_t = _softplus(dt_ref[rows, :].T[:heads, :] + dtb_ref[...])
        da_t = dt_t * (-jnp.exp(alog_ref[...]))
        acs_t = sum(_dot(part, triu) for part in _split3(da_t))
        rowterm_t = (acs_t - jnp.log(dt_t)) * LOG2E
        wgt_t = dt_t * jnp.exp(acs_t[:, L - 1:L] - acs_t)
        acs_n = jnp.concatenate([acs_t * LOG2E, jnp.zeros((LANES - heads, L), F32)], axis=0).T
        ea_hi, ea_lo = _split2(jnp.exp2(acs_n))

        for g in range(groups):
            gcols = slice(g * gw, (g + 1) * gw)
            cg = xbc_ref[rows, c_off + g * nstate:c_off + (g + 1) * nstate]
            bg = xbc_ref[rows, b_off + g * nstate:b_off + (g + 1) * nstate]
            cb = _dot_nt(cg, bg)
            bgt = bg.astype(F32).T
            ea_x = _dot(ea_hi, e_ref[:, gcols]) + _dot(ea_lo, e_ref[:, gcols])
            yacc_ref[g] = _dot(cg, state_ref[g].astype(BF16)) * ea_x
            ssq = jnp.zeros((L, LANES), F32)
            for j in range(pairs):
                h0 = (g * pairs + j) * 2
                tile = slice(j * LANES, (j + 1) * LANES)
                gtile = slice(g * gw + j * LANES, g * gw + (j + 1) * LANES)
                top, bot = [], []
                for h in (h0, h0 + 1):
                    seg = acs_n[:, h:h + 1] - rowterm_t[h:h + 1, :]
                    top.append(jnp.exp2(jnp.where(causal, seg, -jnp.inf)) * cb)
                    bot.append(bgt * wgt_t[h:h + 1, :])
                lhs = jnp.concatenate([jnp.concatenate(top, axis=1), jnp.concatenate(bot, axis=1)],
                                      axis=0).astype(BF16)
                xp = xbc_ref[rows, gtile]
                zero = jnp.zeros_like(xp)
                rhs = jnp.concatenate([jnp.where(lo_half, xp, zero), jnp.where(lo_half, zero, xp)], axis=0)
                res = _dot(lhs, rhs)
                state_ref[g, :, tile] = state_ref[g, :, tile] * ea_x[L - 1:L, tile] + res[L:, :]
                y = res[:L, :] + yacc_ref[g, :, tile] + xp.astype(F32) * dexp_ref[:, gtile]
                y = y * zs_ref[rows, gtile].astype(F32)
                yacc_ref[g, :, tile] = y
                ssq = ssq + y * y
            scale = lax.rsqrt(jnp.sum(ssq, axis=-1, keepdims=True) * (1.0 / gw) + EPS)
            o_ref[rows, gcols] = (yacc_ref[g] * scale * ng_ref[:, gcols]).astype(o_ref.dtype)
        return carry

    lax.fori_loop(0, zs_ref.shape[0] // L, chunk, 0)


def _ssd_branch(zs, xbc, dt_raw, dt_bias, a_log, d_skip, norm_g, bsz, seq):
    t, width = zs.shape
    conv_ch = xbc.shape[1]
    heads = dt_bias.shape[0]
    groups = SSD_GROUPS
    nstate = SSD_STATE
    L = SSD_CHUNK
    rows = _pick(seq, (4 * L, 2 * L, L))
    nb = seq // rows
    dtb = jnp.broadcast_to(dt_bias[:, None], (heads, L))
    alog = jnp.broadcast_to(a_log[:, None], (heads, L))
    dexp = jnp.repeat(d_skip, SSD_HEAD_DIM).reshape(1, width)
    expand = (jnp.arange(LANES)[:, None] == (jnp.arange(width) // SSD_HEAD_DIM)[None, :]).astype(BF16)
    row = lambda b, c: (b * nb + c, 0)
    const = lambda b, c: (0, 0)
    return pl.pallas_call(
        functools.partial(_ssd_kernel, width=width, groups=groups, nstate=nstate, heads=heads),
        grid=(bsz, nb),
        in_specs=[pl.BlockSpec((rows, width), row),
                  pl.BlockSpec((rows, conv_ch), row),
                  pl.BlockSpec((rows, LANES), row),
                  pl.BlockSpec((heads, L), const),
                  pl.BlockSpec((heads, L), const),
                  pl.BlockSpec((1, width), const),
                  pl.BlockSpec((1, width), const),
                  pl.BlockSpec((LANES, width), const)],
        out_specs=pl.BlockSpec((rows, width), row),
        out_shape=jax.ShapeDtypeStruct((t, width), BF16),
        scratch_shapes=[pltpu.VMEM((groups, nstate, width // groups), F32),
                        pltpu.VMEM((groups, L, width // groups), F32)],
        compiler_params=_cparams(("arbitrary", "arbitrary")),
        name="ssd",
    )(zs, xbc, dt_raw, dtb, alog, dexp, norm_g.reshape(1, width), expand)


def _lru_kernel(slg_ref, xl_ref, wg_ref, ba_ref, bx_ref, lam_ref, o_ref, a_ref, u_ref, carry_ref, *, rt):
    c = pl.program_id(1)
    rows, width = xl_ref.shape
    nt = width // LANES
    tg = width // LRU_TILE_GROUPS
    tiles_per_group = tg // LANES

    @pl.when(c == 0)
    def _():
        carry_ref[...] = jnp.zeros(carry_ref.shape, F32)

    def gate_tile(ti, carry):
        rs = pl.ds(pl.multiple_of(ti * rt, rt), rt)
        for g in range(LRU_TILE_GROUPS):
            gs = slice(g * tg, (g + 1) * tg)
            xl = xl_ref[rs, gs]
            gg = _dot(xl, wg_ref[g])
            r = _sigmoid(gg[:, :tg] + ba_ref[:, gs])
            i = _sigmoid(gg[:, tg:] + bx_ref[:, gs])
            a = jnp.exp2(r * ((-LRU_C * LOG2E) * _softplus(-lam_ref[:, gs])))
            u = _sqrt_nonneg(1.0 - a * a) * (i * xl.astype(F32))
            for k in range(tiles_per_group):
                a_ref[g * tiles_per_group + k, rs, :] = a[:, k * LANES:(k + 1) * LANES]
                u_ref[g * tiles_per_group + k, rs, :] = u[:, k * LANES:(k + 1) * LANES]
        return carry

    lax.fori_loop(0, rows // rt, gate_tile, 0)

    sub = lax.broadcasted_iota(jnp.int32, (SUBLANES, LANES), 0)

    def mini(m, carry):
        base = pl.multiple_of(m * SCAN_ROWS, SCAN_ROWS)
        for t in range(nt):
            cprev = carry_ref[:, t * LANES:(t + 1) * LANES]
            av = [a_ref[t, pl.ds(base + j, SUBLANES, stride=SCAN_STRIDE), :] for j in range(SCAN_STRIDE)]
            uv = [u_ref[t, pl.ds(base + j, SUBLANES, stride=SCAN_STRIDE), :] for j in range(SCAN_STRIDE)]
            h, p = uv[0], av[0]
            for j in range(1, SCAN_STRIDE):
                h = av[j] * h + uv[j]
                p = p * av[j]
            d = 1
            while d < SUBLANES:
                hs = pltpu.roll(h, d, 0)
                ps = pltpu.roll(p, d, 0)
                keep = sub < d
                h = jnp.where(keep, h, p * hs + h)
                p = jnp.where(keep, p, p * ps)
                d *= 2
            e = p * cprev + h
            hh = jnp.where(sub == 0, cprev, pltpu.roll(e, 1, 0))
            for j in range(SCAN_STRIDE):
                hh = av[j] * hh + uv[j]
                u_ref[t, pl.ds(base + j, SUBLANES, stride=SCAN_STRIDE), :] = hh
            carry_ref[:, t * LANES:(t + 1) * LANES] = jnp.broadcast_to(e[SUBLANES - 1:SUBLANES, :],
                                                                       (SUBLANES, LANES))
        return carry

    lax.fori_loop(0, rows // SCAN_ROWS, mini, 0)

    for t in range(nt):
        ts = slice(t * LANES, (t + 1) * LANES)
        o_ref[:, ts] = (u_ref[t] * slg_ref[:, ts].astype(F32)).astype(o_ref.dtype)


def _lru_branch(slg, xl, w_a, b_a, w_x, b_x, lam, bsz, seq):
    t, width = xl.shape
    rows = _pick(seq, (512, 256, 128))
    nb = seq // rows
    nblk, blk = w_a.shape[0], w_a.shape[1]
    per = nblk // LRU_TILE_GROUPS
    tg = width // LRU_TILE_GROUPS

    def tile_diag(w):
        wt = w.reshape(LRU_TILE_GROUPS, per, blk, blk)
        eye = jnp.eye(per, dtype=w.dtype)
        return jnp.einsum("gpij,pq->gpiqj", wt, eye).reshape(LRU_TILE_GROUPS, tg, tg)

    wg = jnp.concatenate([tile_diag(w_a), tile_diag(w_x)], axis=2).astype(BF16)
    row = lambda b, c: (b * nb + c, 0)
    const = lambda b, c: (0, 0)
    return pl.pallas_call(
        functools.partial(_lru_kernel, rt=min(rows, 128)),
        grid=(bsz, nb),
        in_specs=[pl.BlockSpec((rows, width), row),
                  pl.BlockSpec((rows, width), row),
                  pl.BlockSpec((LRU_TILE_GROUPS, tg, 2 * tg), lambda b, c: (0, 0, 0)),
                  pl.BlockSpec((1, width), const),
                  pl.BlockSpec((1, width), const),
                  pl.BlockSpec((1, width), const)],
        out_specs=pl.BlockSpec((rows, width), row),
        out_shape=jax.ShapeDtypeStruct((t, width), BF16),
        scratch_shapes=[pltpu.VMEM((width // LANES, rows, LANES), F32),
                        pltpu.VMEM((width // LANES, rows, LANES), F32),
                        pltpu.VMEM((SUBLANES, width), F32)],
        compiler_params=_cparams(("arbitrary", "arbitrary")),
        name="lru",
    )(slg, xl, wg, b_a.reshape(1, width), b_x.reshape(1, width), lam.reshape(1, width))


def _kv_kernel(m_ref, g_ref, w_ref, o_ref):
    m = m_ref[...]
    ms = jnp.mean(m * m, axis=-1, keepdims=True)
    mn = (m * lax.rsqrt(ms + EPS) * g_ref[...]).astype(BF16)
    o_ref[...] = _dot(mn, w_ref[...]).astype(o_ref.dtype)


def _mem_kv(mem2d, g, w_kv_b, tn):
    m, d = mem2d.shape
    n = w_kv_b.shape[1]
    return pl.pallas_call(
        _kv_kernel,
        grid=(n // tn,),
        in_specs=[pl.BlockSpec((m, d), lambda j: (0, 0)),
                  pl.BlockSpec((1, d), lambda j: (0, 0)),
                  pl.BlockSpec((d, tn), lambda j: (0, j))],
        out_specs=pl.BlockSpec((m, tn), lambda j: (0, j)),
        out_shape=jax.ShapeDtypeStruct((m, n), BF16),
        compiler_params=_cparams(("parallel",)),
        name="mem_kv",
    )(mem2d, g.reshape(1, d), w_kv_b)


def _attn_kernel(q_ref, kv_ref, o_ref):
    d = q_ref.shape[1]
    hd = d // MEM_HEADS
    scale = hd ** -0.5
    for h in range(MEM_HEADS):
        hs = slice(h * hd, (h + 1) * hd)
        s = _dot_nt(q_ref[:, hs], kv_ref[:, hs]) * scale
        p = jnp.exp(s - jnp.max(s, axis=-1, keepdims=True))
        l = jnp.sum(p, axis=-1, keepdims=True)
        o_ref[:, hs] = (_dot(p.astype(BF16), kv_ref[:, d + h * hd:d + (h + 1) * hd]) / l).astype(o_ref.dtype)


def _mem_attention(q, kv, bsz, seq, tq):
    t, d = q.shape
    m = kv.shape[0] // bsz
    nq = seq // tq
    return pl.pallas_call(
        _attn_kernel,
        grid=(bsz, nq),
        in_specs=[pl.BlockSpec((tq, d), lambda b, i: (b * nq + i, 0)),
                  pl.BlockSpec((m, 2 * d), lambda b, i: (b, 0))],
        out_specs=pl.BlockSpec((tq, d), lambda b, i: (b * nq + i, 0)),
        out_shape=jax.ShapeDtypeStruct((t, d), BF16),
        compiler_params=_cparams(("parallel", "parallel")),
        name="mem_attn",
    )(q, kv)


def _merge_kernel(ys_ref, yl_ref, ym_ref, gt_ref, x_ref, ws_ref, wl_ref, wm_ref, wo_ref, fg_ref, o_ref, *,
                  final_norm):
    d = x_ref.shape[1]
    merged = (gt_ref[:, :d].astype(F32) * _dot(ys_ref[...], ws_ref[...])
              + gt_ref[:, d:2 * d].astype(F32) * _dot(yl_ref[...], wl_ref[...])
              + gt_ref[:, 2 * d:].astype(F32) * _dot(ym_ref[...], wm_ref[...]))
    xn = x_ref[...] + _dot(merged.astype(BF16), wo_ref[...])
    if final_norm:
        ms = jnp.mean(xn * xn, axis=-1, keepdims=True)
        xn = xn * lax.rsqrt(ms + EPS) * fg_ref[...]
    o_ref[...] = xn


def _merge(y_ssd, y_lru, y_mem, gates, x2d, w_s, w_l, w_m, w_o, final_g, final_norm, tm):
    t, d = x2d.shape
    row = lambda i: (i, 0)
    const = lambda i: (0, 0)
    full = lambda a: pl.BlockSpec(a.shape, const)
    return pl.pallas_call(
        functools.partial(_merge_kernel, final_norm=final_norm),
        grid=(t // tm,),
        in_specs=[pl.BlockSpec((tm, y_ssd.shape[1]), row),
                  pl.BlockSpec((tm, y_lru.shape[1]), row),
                  pl.BlockSpec((tm, y_mem.shape[1]), row),
                  pl.BlockSpec((tm, gates.shape[1]), row),
                  pl.BlockSpec((tm, d), row),
                  full(w_s), full(w_l), full(w_m), full(w_o),
                  pl.BlockSpec((1, d), const)],
        out_specs=pl.BlockSpec((tm, d), row),
        out_shape=jax.ShapeDtypeStruct((t, d), F32),
        compiler_params=_cparams(("parallel",)),
        name="merge",
    )(y_ssd, y_lru, y_mem, gates, x2d, w_s, w_l, w_m, w_o, final_g.reshape(1, d))


def kernel(x, mem, norm_g, w_in, ssd_conv_w, ssd_conv_b, ssd_dt_bias, ssd_a_log, ssd_d, ssd_norm_g, lru_conv_w, lru_conv_b, lru_w_a, lru_b_a, lru_w_x, lru_b_x, lru_lambda, mem_norm_g, w_kv, w_br_ssd, w_br_lru, w_br_mem, w_out, final_g):
    bsz, seq, d = x.shape
    depth = norm_g.shape[0]
    t = bsz * seq
    heads = ssd_dt_bias.shape[1]
    ssd_w = heads * SSD_HEAD_DIM
    conv_ch = ssd_conv_w.shape[2]
    lru_w = lru_conv_w.shape[2]
    tm = _pick(t, (512, 256, 128))
    x2d = x.reshape(t, d)
    mem2d = mem.reshape(bsz * mem.shape[1], d)
    front = ssd_w + conv_ch
    tn = lambda n: _pick(n, (1024, 768, 512, 256, 128))
    for l in range(depth):
        h = _rmsnorm_bf16(x2d, norm_g[l], tm)
        w_front = w_in[l][:, :front].astype(BF16)
        w_dt = jnp.pad(w_in[l][:, front:front + heads], ((0, 0), (0, LANES - heads))).astype(BF16)
        w_rest = w_in[l][:, front + heads:].astype(BF16)

        zs = _proj(h, w_front, 0, ssd_w, tn(ssd_w), seq, "proj_z", act="silu")
        xbc = _proj(h, w_front, ssd_w, conv_ch, tn(conv_ch), seq, "proj_xbc", act="silu",
                    conv_w=ssd_conv_w[l], conv_b=ssd_conv_b[l])
        slg = _proj(h, w_rest, 0, lru_w, tn(lru_w), seq, "proj_lg", act="silu")
        xl = _proj(h, w_rest, lru_w, lru_w, tn(lru_w), seq, "proj_lx",
                   conv_w=lru_conv_w[l], conv_b=lru_conv_b[l])
        q, dt_raw = _proj_q_dt(h, w_rest, 2 * lru_w, d, w_dt, seq)
        gates = _proj(h, w_rest, 2 * lru_w + d, 3 * d, tn(3 * d), seq, "proj_gates", act="sigmoid")

        y_ssd = _ssd_branch(zs, xbc, dt_raw, ssd_dt_bias[l], ssd_a_log[l], ssd_d[l],
                            ssd_norm_g[l].reshape(-1), bsz, seq)
        y_lru = _lru_branch(slg, xl, lru_w_a[l], lru_b_a[l].reshape(-1), lru_w_x[l], lru_b_x[l].reshape(-1),
                            lru_lambda[l], bsz, seq)
        kv = _mem_kv(mem2d, mem_norm_g[l], w_kv[l].astype(BF16), _pick(2 * d, (512, 256, 128)))
        y_mem = _mem_attention(q, kv, bsz, seq, _pick(seq, (512, 256, 128)))
        x2d = _merge(y_ssd, y_lru, y_mem, gates, x2d, w_br_ssd[l].astype(BF16), w_br_lru[l].astype(BF16),
                     w_br_mem[l].astype(BF16), w_out[l].astype(BF16), final_g, l == depth - 1, tm)
    return x2d.reshape(bsz, seq, d)
```

```python
import functools

import jax
import jax.numpy as jnp
from jax import lax
from jax.experimental import pallas as pl
from jax.experimental.pallas import tpu as pltpu

F32 = jnp.float32
BF16 = jnp.bfloat16

EPS = 1e-6
CONV_WIDTH = 4
SSD_HEAD_DIM = 64
SSD_GROUPS = 4
SSD_STATE = 128
SSD_CHUNK = 128
LRU_C = 8.0
MEM_HEADS = 4
LANES = 128
SUBLANES = 8
LRU_TILE_GROUPS = 4
SCAN_STRIDE = 4
SCAN_ROWS = SCAN_STRIDE * SUBLANES

VMEM_LIMIT = 56 * 1024 * 1024


def _cparams(sem):
    return pltpu.CompilerParams(dimension_semantics=sem, vmem_limit_bytes=VMEM_LIMIT)


LOG2E = 1.4426950408889634
TINY = 1e-30


def _sigmoid(x):
    return 1.0 / (1.0 + jnp.exp2(x * (-LOG2E)))


def _sqrt_nonneg(x):
    return x * lax.rsqrt(jnp.maximum(x, TINY))


def _silu(x):
    return x * _sigmoid(x)


def _softplus(x):
    return jnp.maximum(x, 0.0) + jnp.log1p(jnp.exp(-jnp.abs(x)))


def _split2(x):
    hi = x.astype(BF16)
    lo = (x - hi.astype(F32)).astype(BF16)
    return hi, lo


def _split3(x):
    hi = x.astype(BF16)
    r = x - hi.astype(F32)
    mid = r.astype(BF16)
    lo = (r - mid.astype(F32)).astype(BF16)
    return hi, mid, lo


def _dot(a, b):
    return jnp.dot(a, b, preferred_element_type=F32)


def _dot_nt(a, b):
    return lax.dot_general(a, b, (((1,), (1,)), ((), ())), preferred_element_type=F32)


def _pick(n, prefs):
    for p in prefs:
        if n % p == 0:
            return p
    return n


def _proj_kernel(*refs, sub, act, conv, cast_w, blocks_per_seq):
    refs = list(refs)
    pad_ref = refs.pop() if conv else None
    wb_ref = refs.pop() if cast_w else None
    acc1_ref = refs.pop()
    acc0_ref = refs.pop()
    o_ref = refs.pop()
    if conv:
        h_ref, w_ref, cw_ref, cb_ref = refs
    else:
        h_ref, w_ref = refs
    acc_refs = (acc0_ref, acc1_ref)
    tm = h_ref.shape[0]
    nsub = tm // sub
    i = pl.program_id(1)

    if cast_w:
        @pl.when(i == 0)
        def _():
            wb_ref[...] = w_ref[...].astype(BF16)

        w_ref = wb_ref

    if conv:
        @pl.when(i % blocks_per_seq == 0)
        def _():
            pad_ref[:, 0:2 * SUBLANES, :] = jnp.zeros((pad_ref.shape[0], 2 * SUBLANES, LANES), F32)

    def epilogue(r):
        rows = slice(r * sub, (r + 1) * sub)
        acc_ref = acc_refs[r % 2]
        if not conv:
            acc = acc_ref[...]
            o_ref[rows, :] = (_silu(acc) if act == "silu" else _sigmoid(acc)).astype(o_ref.dtype)
            return
        for c in range(pad_ref.shape[0]):
            cols = slice(c * LANES, (c + 1) * LANES)
            x = acc_ref[:, cols]
            pad_ref[c, pl.ds(2 * SUBLANES, sub, stride=2), :] = x
            y = cb_ref[:, cols] + cw_ref[CONV_WIDTH - 1:CONV_WIDTH, cols] * x
            for s in range(1, CONV_WIDTH):
                y = y + (cw_ref[CONV_WIDTH - 1 - s:CONV_WIDTH - s, cols]
                         * pad_ref[c, pl.ds(2 * (SUBLANES - s), sub, stride=2), :])
            pad_ref[c, pl.ds(0, SUBLANES, stride=2), :] = x[sub - SUBLANES:, :]
            o_ref[rows, cols] = (_silu(y) if act == "silu" else y).astype(o_ref.dtype)

    for r in range(nsub + 1):
        if r > 0:
            epilogue(r - 1)
        if r < nsub:
            acc_refs[r % 2][...] = _dot(h_ref[r * sub:(r + 1) * sub, :], w_ref[...])


def _proj(h, w, col0, ncols, tn, seq, name, act=None, conv_w=None, conv_b=None, out_dtype=BF16):
    t, k = h.shape
    tm = _pick(seq, (2048, 1024, 512, 256, 128))
    sub = min(tm, 256)
    conv = conv_w is not None
    joff = col0 // tn
    in_specs = [pl.BlockSpec((tm, k), lambda j, i: (i, 0)),
                pl.BlockSpec((k, tn), lambda j, i: (0, j + joff))]
    args = [h, w]
    cast_w = w.dtype != BF16
    scratch = [pltpu.VMEM((sub, tn), F32), pltpu.VMEM((sub, tn), F32)]
    if cast_w:
        scratch += [pltpu.VMEM((k, tn), BF16)]
    if conv:
        in_specs += [pl.BlockSpec((CONV_WIDTH, tn), lambda j, i: (0, j)),
                     pl.BlockSpec((1, tn), lambda j, i: (0, j))]
        args += [conv_w, conv_b.reshape(1, ncols)]
        scratch += [pltpu.VMEM((tn // LANES, 2 * (sub + SUBLANES), LANES), F32)]
    return pl.pallas_call(
        functools.partial(_proj_kernel, sub=sub, act=act, conv=conv, cast_w=cast_w,
                          blocks_per_seq=seq // tm),
        grid=(ncols // tn, t // tm),
        in_specs=in_specs,
        out_specs=pl.BlockSpec((tm, tn), lambda j, i: (i, j)),
        out_shape=jax.ShapeDtypeStruct((t, ncols), out_dtype),
        scratch_shapes=scratch,
        compiler_params=_cparams(("arbitrary", "arbitrary")),
        name=name,
    )(*args)


def _norm_q_dt_kernel(x_ref, g_ref, wq_ref, wdt_ref, h_ref, q_ref, dt_ref, *, sub):
    tm = x_ref.shape[0]
    for r in range(tm // sub):
        rows = slice(r * sub, (r + 1) * sub)
        x = x_ref[rows, :]
        ms = jnp.mean(x * x, axis=-1, keepdims=True)
        h = (x * lax.rsqrt(ms + EPS) * g_ref[...]).astype(BF16)
        h_ref[rows, :] = h
        q_ref[rows, :] = _dot(h, wq_ref[...]).astype(q_ref.dtype)
        dt_ref[rows, :] = _dot(h, wdt_ref[...])


def _norm_q_dt(x2d, g, w, col0, ncols, w_dt, seq):
    t, k = x2d.shape
    tm = _pick(seq, (1024, 512, 256, 128))
    sub = min(tm, 256)
    joff = col0 // ncols
    return pl.pallas_call(
        functools.partial(_norm_q_dt_kernel, sub=sub),
        grid=(t // tm,),
        in_specs=[pl.BlockSpec((tm, k), lambda i: (i, 0)),
                  pl.BlockSpec((1, k), lambda i: (0, 0)),
                  pl.BlockSpec((k, ncols), lambda i: (0, joff)),
                  pl.BlockSpec((k, LANES), lambda i: (0, 0))],
        out_specs=[pl.BlockSpec((tm, k), lambda i: (i, 0)),
                   pl.BlockSpec((tm, ncols), lambda i: (i, 0)),
                   pl.BlockSpec((tm, LANES), lambda i: (i, 0))],
        out_shape=[jax.ShapeDtypeStruct((t, k), BF16),
                   jax.ShapeDtypeStruct((t, ncols), BF16),
                   jax.ShapeDtypeStruct((t, LANES), F32)],
        compiler_params=_cparams(("parallel",)),
        name="norm_q_dt",
    )(x2d, g.reshape(1, k), w, w_dt)


def _ssd_kernel(zs_ref, xbc_ref, dt_ref, dtb_ref, alog_ref, dexp_ref, ng_ref, e_ref,
                o_ref, state_ref, yacc_ref, *, width, groups, nstate, heads):
    c = pl.program_id(1)
    L = SSD_CHUNK
    gw = width // groups
    pairs = gw // LANES
    b_off = width
    c_off = width + groups * nstate

    @pl.when(c == 0)
    def _():
        state_ref[...] = jnp.zeros(state_ref.shape, F32)

    row = lax.broadcasted_iota(jnp.int32, (L, L), 0)
    col = lax.broadcasted_iota(jnp.int32, (L, L), 1)
    causal = row >= col
    triu = (row <= col).astype(BF16)
    lo_half = lax.broadcasted_iota(jnp.int32, (L, LANES), 1) < SSD_HEAD_DIM

    def chunk(ci, carry):
        rows = pl.ds(pl.multiple_of(ci * L, L), L)
        dt_t = _softplus(dt_ref[rows, :].T[:heads, :] + dtb_ref[...])
        da_t = dt_t * (-jnp.exp(alog_ref[...]))
        acs_t = sum(_dot(part, triu) for part in _split3(da_t))
        rowterm_t = (acs_t - jnp.log(dt_t)) * LOG2E
        wgt_t = dt_t * jnp.exp(acs_t[:, L - 1:L] - acs_t)
        acs_n = jnp.concatenate([acs_t * LOG2E, jnp.zeros((LANES - heads, L), F32)], axis=0).T
        ea_hi, ea_lo = _split2(jnp.exp2(acs_n))

        for g in range(groups):
            gcols = slice(g * gw, (g + 1) * gw)
            cg = xbc_ref[rows, c_off + g * nstate:c_off + (g + 1) * nstate]
            bg = xbc_ref[rows, b_off + g * nstate:b_off + (g + 1) * nstate]
            cb = _dot_nt(cg, bg)
            bgt = bg.astype(F32).T
            ea_x = _dot(ea_hi, e_ref[:, gcols]) + _dot(ea_lo, e_ref[:, gcols])
            yacc_ref[g] = _dot(cg, state_ref[g].astype(BF16)) * ea_x
            ssq = jnp.zeros((L, LANES), F32)
            for j in range(pairs):
                h0 = (g * pairs + j) * 2
                tile = slice(j * LANES, (j + 1) * LANES)
                gtile = slice(g * gw + j * LANES, g * gw + (j + 1) * LANES)
                top, bot = [], []
                for h in (h0, h0 + 1):
                    seg = acs_n[:, h:h + 1] - rowterm_t[h:h + 1, :]
                    top.append(jnp.exp2(jnp.where(causal, seg, -jnp.inf)) * cb)
                    bot.append(bgt * wgt_t[h:h + 1, :])
                lhs = jnp.concatenate([jnp.concatenate(top, axis=1), jnp.concatenate(bot, axis=1)],
                                      axis=0).astype(BF16)
                xp = xbc_ref[rows, gtile]
                zero = jnp.zeros_like(xp)
                rhs = jnp.concatenate([jnp.where(lo_half, xp, zero), jnp.where(lo_half, zero, xp)], axis=0)
                res = _dot(lhs, rhs)
                state_ref[g, :, tile] = state_ref[g, :, tile] * ea_x[L - 1:L, tile] + res[L:, :]
                y = res[:L, :] + yacc_ref[g, :, tile] + xp.astype(F32) * dexp_ref[:, gtile]
                y = y * zs_ref[rows, gtile].astype(F32)
                yacc_ref[g, :, tile] = y
                ssq = ssq + y * y
            scale = lax.rsqrt(jnp.sum(ssq, axis=-1, keepdims=True) * (1.0 / gw) + EPS)
            o_ref[rows, gcols] = (yacc_ref[g] * scale * ng_ref[:, gcols]).astype(o_ref.dtype)
        return carry

    lax.fori_loop(0, zs_ref.shape[0] // L, chunk, 0, unroll=2)


def _ssd_branch(zs, xbc, dt_raw, dt_bias, a_log, d_skip, norm_g, bsz, seq):
    t, width = zs.shape
    conv_ch = xbc.shape[1]
    heads = dt_bias.shape[0]
    groups = SSD_GROUPS
    nstate = SSD_STATE
    L = SSD_CHUNK
    rows = _pick(seq, (4 * L, 2 * L, L))
    nb = seq // rows
    dtb = jnp.broadcast_to(dt_bias[:, None], (heads, L))
    alog = jnp.broadcast_to(a_log[:, None], (heads, L))
    dexp = jnp.repeat(d_skip, SSD_HEAD_DIM).reshape(1, width)
    expand = (jnp.arange(LANES)[:, None] == (jnp.arange(width) // SSD_HEAD_DIM)[None, :]).astype(BF16)
    row = lambda b, c: (b * nb + c, 0)
    const = lambda b, c: (0, 0)
    return pl.pallas_call(
        functools.partial(_ssd_kernel, width=width, groups=groups, nstate=nstate, heads=heads),
        grid=(bsz, nb),
        in_specs=[pl.BlockSpec((rows, width), row),
                  pl.BlockSpec((rows, conv_ch), row),
                  pl.BlockSpec((rows, LANES), row),
                  pl.BlockSpec((heads, L), const),
                  pl.BlockSpec((heads, L), const),
                  pl.BlockSpec((1, width), const),
                  pl.BlockSpec((1, width), const),
                  pl.BlockSpec((LANES, width), const)],
        out_specs=pl.BlockSpec((rows, width), row),
        out_shape=jax.ShapeDtypeStruct((t, width), BF16),
        scratch_shapes=[pltpu.VMEM((groups, nstate, width // groups), F32),
                        pltpu.VMEM((groups, L, width // groups), F32)],
        compiler_params=_cparams(("arbitrary", "arbitrary")),
        name="ssd",
    )(zs, xbc, dt_raw, dtb, alog, dexp, norm_g.reshape(1, width), expand)


def _lru_kernel(slg_ref, xl_ref, wg_ref, ba_ref, bx_ref, lam_ref, o_ref, a_ref, u_ref, carry_ref, *, rt):
    c = pl.program_id(1)
    rows, width = xl_ref.shape
    nt = width // LANES
    tg = width // LRU_TILE_GROUPS
    tiles_per_group = tg // LANES

    @pl.when(c == 0)
    def _():
        carry_ref[...] = jnp.zeros(carry_ref.shape, F32)

    def gate_tile(ti, carry):
        rs = pl.ds(pl.multiple_of(ti * rt, rt), rt)
        for g in range(LRU_TILE_GROUPS):
            gs = slice(g * tg, (g + 1) * tg)
            xl = xl_ref[rs, gs]
            gg = _dot(xl, wg_ref[g])
            r = _sigmoid(gg[:, :tg] + ba_ref[:, gs])
            i = _sigmoid(gg[:, tg:] + bx_ref[:, gs])
            a = jnp.exp2(r * ((-LRU_C * LOG2E) * _softplus(-lam_ref[:, gs])))
            u = _sqrt_nonneg(1.0 - a * a) * (i * xl.astype(F32))
            for k in range(tiles_per_group):
                a_ref[g * tiles_per_group + k, rs, :] = a[:, k * LANES:(k + 1) * LANES]
                u_ref[g * tiles_per_group + k, rs, :] = u[:, k * LANES:(k + 1) * LANES]
        return carry

    lax.fori_loop(0, rows // rt, gate_tile, 0, unroll=2)

    sub = lax.broadcasted_iota(jnp.int32, (SUBLANES, LANES), 0)

    def mini(m, carry):
        base = pl.multiple_of(m * SCAN_ROWS, SCAN_ROWS)
        for t in range(nt):
            cprev = carry_ref[:, t * LANES:(t + 1) * LANES]
            av = [a_ref[t, pl.ds(base + j, SUBLANES, stride=SCAN_STRIDE), :] for j in range(SCAN_STRIDE)]
            uv = [u_ref[t, pl.ds(base + j, SUBLANES, stride=SCAN_STRIDE), :] for j in range(SCAN_STRIDE)]
            h, p = uv[0], av[0]
            for j in range(1, SCAN_STRIDE):
                h = av[j] * h + uv[j]
                p = p * av[j]
            d = 1
            while d < SUBLANES:
                hs = pltpu.roll(h, d, 0)
                ps = pltpu.roll(p, d, 0)
                keep = sub < d
                h = jnp.where(keep, h, p * hs + h)
                p = jnp.where(keep, p, p * ps)
                d *= 2
            e = p * cprev + h
            hh = jnp.where(sub == 0, cprev, pltpu.roll(e, 1, 0))
            for j in range(SCAN_STRIDE):
                hh = av[j] * hh + uv[j]
                u_ref[t, pl.ds(base + j, SUBLANES, stride=SCAN_STRIDE), :] = hh
            carry_ref[:, t * LANES:(t + 1) * LANES] = jnp.broadcast_to(e[SUBLANES - 1:SUBLANES, :],
                                                                       (SUBLANES, LANES))
        return carry

    lax.fori_loop(0, rows // SCAN_ROWS, mini, 0)

    for t in range(nt):
        ts = slice(t * LANES, (t + 1) * LANES)
        o_ref[:, ts] = (u_ref[t] * slg_ref[:, ts].astype(F32)).astype(o_ref.dtype)


def _lru_branch(slg, xl, w_a, b_a, w_x, b_x, lam, bsz, seq):
    t, width = xl.shape
    rows = _pick(seq, (512, 256, 128))
    nb = seq // rows
    nblk, blk = w_a.shape[0], w_a.shape[1]
    per = nblk // LRU_TILE_GROUPS
    tg = width // LRU_TILE_GROUPS

    def tile_diag(w):
        wt = w.reshape(LRU_TILE_GROUPS, per, blk, blk)
        eye = jnp.eye(per, dtype=w.dtype)
        return jnp.einsum("gpij,pq->gpiqj", wt, eye).reshape(LRU_TILE_GROUPS, tg, tg)

    wg = jnp.concatenate([tile_diag(w_a), tile_diag(w_x)], axis=2).astype(BF16)
    row = lambda b, c: (b * nb + c, 0)
    const = lambda b, c: (0, 0)
    return pl.pallas_call(
        functools.partial(_lru_kernel, rt=min(rows, 128)),
        grid=(bsz, nb),
        in_specs=[pl.BlockSpec((rows, width), row),
                  pl.BlockSpec((rows, width), row),
                  pl.BlockSpec((LRU_TILE_GROUPS, tg, 2 * tg), lambda b, c: (0, 0, 0)),
                  pl.BlockSpec((1, width), const),
                  pl.BlockSpec((1, width), const),
                  pl.BlockSpec((1, width), const)],
        out_specs=pl.BlockSpec((rows, width), row),
        out_shape=jax.ShapeDtypeStruct((t, width), BF16),
        scratch_shapes=[pltpu.VMEM((width // LANES, rows, LANES), F32),
                        pltpu.VMEM((width // LANES, rows, LANES), F32),
                        pltpu.VMEM((SUBLANES, width), F32)],
        compiler_params=_cparams(("arbitrary", "arbitrary")),
        name="lru",
    )(slg, xl, wg, b_a.reshape(1, width), b_x.reshape(1, width), lam.reshape(1, width))


def _kv_kernel(m_ref, g_ref, w_ref, o_ref):
    m = m_ref[...]
    ms = jnp.mean(m * m, axis=-1, keepdims=True)
    mn = (m * lax.rsqrt(ms + EPS) * g_ref[...]).astype(BF16)
    o_ref[...] = _dot(mn, w_ref[...]).astype(o_ref.dtype)


def _mem_kv(mem2d, g, w_kv_b, tn):
    m, d = mem2d.shape
    n = w_kv_b.shape[1]
    return pl.pallas_call(
        _kv_kernel,
        grid=(n // tn,),
        in_specs=[pl.BlockSpec((m, d), lambda j: (0, 0)),
                  pl.BlockSpec((1, d), lambda j: (0, 0)),
                  pl.BlockSpec((d, tn), lambda j: (0, j))],
        out_specs=pl.BlockSpec((m, tn), lambda j: (0, j)),
        out_shape=jax.ShapeDtypeStruct((m, n), BF16),
        compiler_params=_cparams(("parallel",)),
        name="mem_kv",
    )(mem2d, g.reshape(1, d), w_kv_b)


def _attn_kernel(q_ref, kv_ref, o_ref):
    d = q_ref.shape[1]
    hd = d // MEM_HEADS
    scale = hd ** -0.5
    for h in range(MEM_HEADS):
        hs = slice(h * hd, (h + 1) * hd)
        s = _dot_nt(q_ref[:, hs], kv_ref[:, hs]) * scale
        p = jnp.exp(s - jnp.max(s, axis=-1, keepdims=True))
        l = jnp.sum(p, axis=-1, keepdims=True)
        o_ref[:, hs] = (_dot(p.astype(BF16), kv_ref[:, d + h * hd:d + (h + 1) * hd]) / l).astype(o_ref.dtype)


def _mem_attention(q, kv, bsz, seq, tq):
    t, d = q.shape
    m = kv.shape[0] // bsz
    nq = seq // tq
    return pl.pallas_call(
        _attn_kernel,
        grid=(bsz, nq),
        in_specs=[pl.BlockSpec((tq, d), lambda b, i: (b * nq + i, 0)),
                  pl.BlockSpec((m, 2 * d), lambda b, i: (b, 0))],
        out_specs=pl.BlockSpec((tq, d), lambda b, i: (b * nq + i, 0)),
        out_shape=jax.ShapeDtypeStruct((t, d), BF16),
        compiler_params=_cparams(("parallel", "parallel")),
        name="mem_attn",
    )(q, kv)


def _merge_kernel(ys_ref, yl_ref, ym_ref, gt_ref, x_ref, ws_ref, wl_ref, wm_ref, wo_ref, fg_ref, o_ref, *,
                  final_norm):
    d = x_ref.shape[1]
    merged = (gt_ref[:, :d].astype(F32) * _dot(ys_ref[...], ws_ref[...])
              + gt_ref[:, d:2 * d].astype(F32) * _dot(yl_ref[...], wl_ref[...])
              + gt_ref[:, 2 * d:].astype(F32) * _dot(ym_ref[...], wm_ref[...]))
    xn = x_ref[...] + _dot(merged.astype(BF16), wo_ref[...])
    if final_norm:
        ms = jnp.mean(xn * xn, axis=-1, keepdims=True)
        xn = xn * lax.rsqrt(ms + EPS) * fg_ref[...]
    o_ref[...] = xn


def _merge(y_ssd, y_lru, y_mem, gates, x2d, w_s, w_l, w_m, w_o, final_g, final_norm, tm):
    t, d = x2d.shape
    row = lambda i: (i, 0)
    const = lambda i: (0, 0)
    full = lambda a: pl.BlockSpec(a.shape, const)
    return pl.pallas_call(
        functools.partial(_merge_kernel, final_norm=final_norm),
        grid=(t // tm,),
        in_specs=[pl.BlockSpec((tm, y_ssd.shape[1]), row),
                  pl.BlockSpec((tm, y_lru.shape[1]), row),
                  pl.BlockSpec((tm, y_mem.shape[1]), row),
                  pl.BlockSpec((tm, gates.shape[1]), row),
                  pl.BlockSpec((tm, d), row),
                  full(w_s), full(w_l), full(w_m), full(w_o),
                  pl.BlockSpec((1, d), const)],
        out_specs=pl.BlockSpec((tm, d), row),
        out_shape=jax.ShapeDtypeStruct((t, d), F32),
        compiler_params=_cparams(("parallel",)),
        name="merge",
    )(y_ssd, y_lru, y_mem, gates, x2d, w_s, w_l, w_m, w_o, final_g.reshape(1, d))


def kernel(x, mem, norm_g, w_in, ssd_conv_w, ssd_conv_b, ssd_dt_bias, ssd_a_log, ssd_d, ssd_norm_g, lru_conv_w, lru_conv_b, lru_w_a, lru_b_a, lru_w_x, lru_b_x, lru_lambda, mem_norm_g, w_kv, w_br_ssd, w_br_lru, w_br_mem, w_out, final_g):
    bsz, seq, d = x.shape
    depth = norm_g.shape[0]
    t = bsz * seq
    heads = ssd_dt_bias.shape[1]
    ssd_w = heads * SSD_HEAD_DIM
    conv_ch = ssd_conv_w.shape[2]
    lru_w = lru_conv_w.shape[2]
    tm = _pick(t, (512, 256, 128))
    x2d = x.reshape(t, d)
    mem2d = mem.reshape(bsz * mem.shape[1], d)
    front = ssd_w + conv_ch
    tn = lambda n: _pick(n, (1024, 768, 512, 256, 128))
    for l in range(depth):
        w_dt = jnp.pad(w_in[l][:, front:front + heads], ((0, 0), (0, LANES - heads))).astype(BF16)
        w_rest = w_in[l][:, front + heads:].astype(BF16)

        h, q, dt_raw = _norm_q_dt(x2d, norm_g[l], w_rest, 2 * lru_w, d, w_dt, seq)
        zs = _proj(h, w_in[l], 0, ssd_w, tn(ssd_w), seq, "proj_z", act="silu")
        xbc = _proj(h, w_in[l], ssd_w, conv_ch, tn(conv_ch), seq, "proj_xbc", act="silu",
                    conv_w=ssd_conv_w[l], conv_b=ssd_conv_b[l])
        slg = _proj(h, w_rest, 0, lru_w, tn(lru_w), seq, "proj_lg", act="silu")
        xl = _proj(h, w_rest, lru_w, lru_w, tn(lru_w), seq, "proj_lx",
                   conv_w=lru_conv_w[l], conv_b=lru_conv_b[l])
        gates = _proj(h, w_rest, 2 * lru_w + d, 3 * d, tn(3 * d), seq, "proj_gates", act="sigmoid")

        y_ssd = _ssd_branch(zs, xbc, dt_raw, ssd_dt_bias[l], ssd_a_log[l], ssd_d[l],
                            ssd_norm_g[l].reshape(-1), bsz, seq)
        y_lru = _lru_branch(slg, xl, lru_w_a[l], lru_b_a[l].reshape(-1), lru_w_x[l], lru_b_x[l].reshape(-1),
                            lru_lambda[l], bsz, seq)
        kv = _mem_kv(mem2d, mem_norm_g[l], w_kv[l].astype(BF16), _pick(2 * d, (512, 256, 128)))
        y_mem = _mem_attention(q, kv, bsz, seq, _pick(seq, (512, 256, 128)))
        x2d = _merge(y_ssd, y_lru, y_mem, gates, x2d, w_br_ssd[l].astype(BF16), w_br_lru[l].astype(BF16),
                     w_br_mem[l].astype(BF16), w_out[l].astype(BF16), final_g, l == depth - 1, tm)
    return x2d.reshape(bsz, seq, d)
```

```python
import functools

import jax
import jax.numpy as jnp
from jax import lax
from jax.experimental import pallas as pl
from jax.experimental.pallas import tpu as pltpu

F32 = jnp.float32
BF16 = jnp.bfloat16

EPS = 1e-6
CONV_WIDTH = 4
SSD_HEAD_DIM = 64
SSD_GROUPS = 4
SSD_STATE = 128
SSD_CHUNK = 128
LRU_C = 8.0
MEM_HEADS = 4
LANES = 128
SUBLANES = 8
LRU_TILE_GROUPS = 4
SCAN_STRIDE = 4
SCAN_ROWS = SCAN_STRIDE * SUBLANES

VMEM_LIMIT = 56 * 1024 * 1024


def _cparams(sem):
    return pltpu.CompilerParams(dimension_semantics=sem, vmem_limit_bytes=VMEM_LIMIT)


LOG2E = 1.4426950408889634
TINY = 1e-30


def _sigmoid(x):
    return 1.0 / (1.0 + jnp.exp2(x * (-LOG2E)))


def _sigmoid_tanh(x):
    return 0.5 * jnp.tanh(0.5 * x) + 0.5


def _sqrt_nonneg(x):
    return x * lax.rsqrt(jnp.maximum(x, TINY))


def _silu(x):
    return x * _sigmoid(x)


def _softplus(x):
    return jnp.maximum(x, 0.0) + jnp.log1p(jnp.exp(-jnp.abs(x)))


def _split2(x):
    hi = x.astype(BF16)
    lo = (x - hi.astype(F32)).astype(BF16)
    return hi, lo


def _split3(x):
    hi = x.astype(BF16)
    r = x - hi.astype(F32)
    mid = r.astype(BF16)
    lo = (r - mid.astype(F32)).astype(BF16)
    return hi, mid, lo


def _dot(a, b):
    return jnp.dot(a, b, preferred_element_type=F32)


def _dot_nt(a, b):
    return lax.dot_general(a, b, (((1,), (1,)), ((), ())), preferred_element_type=F32)


def _pick(n, prefs):
    for p in prefs:
        if n % p == 0:
            return p
    return n


def _proj_kernel(*refs, sub, act, conv, blocks_per_seq):
    if conv:
        h_ref, wt_ref, cw_ref, cb_ref, o_ref, w_ref, pad_ref = refs
    else:
        h_ref, wt_ref, o_ref, w_ref, acc0_ref, acc1_ref = refs
        acc_refs = (acc0_ref, acc1_ref)
    tm = h_ref.shape[0]
    nsub = tm // sub
    i = pl.program_id(1)

    @pl.when(i == 0)
    def _():
        w_ref[...] = wt_ref[...].T.astype(BF16)

    if conv:
        @pl.when(i % blocks_per_seq == 0)
        def _():
            pad_ref[:, 0:2 * SUBLANES, :] = jnp.zeros((pad_ref.shape[0], 2 * SUBLANES, LANES), F32)

        for r in range(nsub):
            rows = slice(r * sub, (r + 1) * sub)
            acc = _dot(h_ref[rows, :], w_ref[...])
            for c in range(pad_ref.shape[0]):
                cols = slice(c * LANES, (c + 1) * LANES)
                x = acc[:, cols]
                pad_ref[c, pl.ds(2 * SUBLANES, sub, stride=2), :] = x
                y = cb_ref[:, cols] + cw_ref[CONV_WIDTH - 1:CONV_WIDTH, cols] * x
                for s in range(1, CONV_WIDTH):
                    y = y + (cw_ref[CONV_WIDTH - 1 - s:CONV_WIDTH - s, cols]
                             * pad_ref[c, pl.ds(2 * (SUBLANES - s), sub, stride=2), :])
                pad_ref[c, pl.ds(0, SUBLANES, stride=2), :] = x[sub - SUBLANES:, :]
                o_ref[rows, cols] = (_silu(y) if act == "silu" else y).astype(o_ref.dtype)
        return

    for r in range(nsub + 1):
        if r > 0:
            acc = acc_refs[(r - 1) % 2][...]
            o_ref[(r - 1) * sub:r * sub, :] = (_silu(acc) if act == "silu" else _sigmoid(acc)).astype(o_ref.dtype)
        if r < nsub:
            acc_refs[r % 2][...] = _dot(h_ref[r * sub:(r + 1) * sub, :], w_ref[...])


def _proj(h, wt, row0, ncols, tn, seq, name, act=None, conv_w=None, conv_b=None, out_dtype=BF16):
    t, k = h.shape
    tm = _pick(seq, (2048, 1024, 512, 256, 128))
    conv = conv_w is not None
    sub = min(tm, 512 if conv else 256)
    joff = row0 // tn
    in_specs = [pl.BlockSpec((tm, k), lambda j, i: (i, 0)),
                pl.BlockSpec((tn, k), lambda j, i: (j + joff, 0))]
    args = [h, wt]
    scratch = [pltpu.VMEM((k, tn), BF16)]
    if conv:
        in_specs += [pl.BlockSpec((CONV_WIDTH, tn), lambda j, i: (0, j)),
                     pl.BlockSpec((1, tn), lambda j, i: (0, j))]
        args += [conv_w, conv_b.reshape(1, ncols)]
        scratch += [pltpu.VMEM((tn // LANES, 2 * (sub + SUBLANES), LANES), F32)]
    else:
        scratch += [pltpu.VMEM((sub, tn), F32), pltpu.VMEM((sub, tn), F32)]
    return pl.pallas_call(
        functools.partial(_proj_kernel, sub=sub, act=act, conv=conv, blocks_per_seq=seq // tm),
        grid=(ncols // tn, t // tm),
        in_specs=in_specs,
        out_specs=pl.BlockSpec((tm, tn), lambda j, i: (i, j)),
        out_shape=jax.ShapeDtypeStruct((t, ncols), out_dtype),
        scratch_shapes=scratch,
        compiler_params=_cparams(("arbitrary", "arbitrary")),
        name=name,
    )(*args)


def _norm_q_dt_kernel(x_ref, g_ref, wqt_ref, wdtt_ref, h_ref, q_ref, dt_ref, wq_ref, wdt_ref, *, sub):
    tm = x_ref.shape[0]

    @pl.when(pl.program_id(0) == 0)
    def _():
        wq_ref[...] = wqt_ref[...].T.astype(BF16)
        wdt_ref[...] = wdtt_ref[...].T.astype(BF16)

    for r in range(tm // sub):
        rows = slice(r * sub, (r + 1) * sub)
        x = x_ref[rows, :]
        ms = jnp.mean(x * x, axis=-1, keepdims=True)
        h = (x * lax.rsqrt(ms + EPS) * g_ref[...]).astype(BF16)
        h_ref[rows, :] = h
        q_ref[rows, :] = _dot(h, wq_ref[...]).astype(q_ref.dtype)
        dt_ref[rows, :] = _dot(h, wdt_ref[...])


def _norm_q_dt(x2d, g, wt, row0, ncols, wt_dt, seq):
    t, k = x2d.shape
    tm = _pick(seq, (1024, 512, 256, 128))
    sub = min(tm, 256)
    joff = row0 // ncols
    return pl.pallas_call(
        functools.partial(_norm_q_dt_kernel, sub=sub),
        grid=(t // tm,),
        in_specs=[pl.BlockSpec((tm, k), lambda i: (i, 0)),
                  pl.BlockSpec((1, k), lambda i: (0, 0)),
                  pl.BlockSpec((ncols, k), lambda i: (joff, 0)),
                  pl.BlockSpec((LANES, k), lambda i: (0, 0))],
        out_specs=[pl.BlockSpec((tm, k), lambda i: (i, 0)),
                   pl.BlockSpec((tm, ncols), lambda i: (i, 0)),
                   pl.BlockSpec((tm, LANES), lambda i: (i, 0))],
        out_shape=[jax.ShapeDtypeStruct((t, k), BF16),
                   jax.ShapeDtypeStruct((t, ncols), BF16),
                   jax.ShapeDtypeStruct((t, LANES), F32)],
        scratch_shapes=[pltpu.VMEM((k, ncols), BF16), pltpu.VMEM((k, LANES), BF16)],
        compiler_params=_cparams(("arbitrary",)),
        name="norm_q_dt",
    )(x2d, g.reshape(1, k), wt, wt_dt)


def _ssd_kernel(zs_ref, xbc_ref, dt_ref, dtb_ref, alog_ref, dexp_ref, ng_ref, e_ref,
                o_ref, state_ref, yacc_ref, *, width, groups, nstate, heads):
    c = pl.program_id(1)
    L = SSD_CHUNK
    gw = width // groups
    pairs = gw // LANES
    b_off = width
    c_off = width + groups * nstate

    @pl.when(c == 0)
    def _():
        state_ref[...] = jnp.zeros(state_ref.shape, F32)

    row = lax.broadcasted_iota(jnp.int32, (L, L), 0)
    col = lax.broadcasted_iota(jnp.int32, (L, L), 1)
    causal = row >= col
    triu = (row <= col).astype(BF16)
    lo_half = lax.broadcasted_iota(jnp.int32, (L, LANES), 1) < SSD_HEAD_DIM

    def chunk(ci, carry):
        rows = pl.ds(pl.multiple_of(ci * L, L), L)
        dt_t = _softplus(dt_ref[rows, :].T[:heads, :] + dtb_ref[...])
        da_t = dt_t * (-jnp.exp(alog_ref[...]))
        acs_t = sum(_dot(part, triu) for part in _split3(da_t))
        rowterm_t = (acs_t - jnp.log(dt_t)) * LOG2E
        wgt_t = dt_t * jnp.exp(acs_t[:, L - 1:L] - acs_t)
        acs_n = jnp.concatenate([acs_t * LOG2E, jnp.zeros((LANES - heads, L), F32)], axis=0).T
        ea_hi, ea_lo = _split2(jnp.exp2(acs_n))

        for g in range(groups):
            gcols = slice(g * gw, (g + 1) * gw)
            cg = xbc_ref[rows, c_off + g * nstate:c_off + (g + 1) * nstate]
            bg = xbc_ref[rows, b_off + g * nstate:b_off + (g + 1) * nstate]
            cb = _dot_nt(cg, bg)
            bgt = bg.astype(F32).T
            ea_x = _dot(ea_hi, e_ref[:, gcols]) + _dot(ea_lo, e_ref[:, gcols])
            yacc_ref[g] = _dot(cg, state_ref[g].astype(BF16)) * ea_x
            ssq = jnp.zeros((L, LANES), F32)
            for j in range(pairs):
                h0 = (g * pairs + j) * 2
                tile = slice(j * LANES, (j + 1) * LANES)
                gtile = slice(g * gw + j * LANES, g * gw + (j + 1) * LANES)
                top, bot = [], []
                for h in (h0, h0 + 1):
                    seg = acs_n[:, h:h + 1] - rowterm_t[h:h + 1, :]
                    top.append(jnp.exp2(jnp.where(causal, seg, -jnp.inf)) * cb)
                    bot.append(bgt * wgt_t[h:h + 1, :])
                lhs = jnp.concatenate([jnp.concatenate(top, axis=1), jnp.concatenate(bot, axis=1)],
                                      axis=0).astype(BF16)
                xp = xbc_ref[rows, gtile]
                zero = jnp.zeros_like(xp)
                rhs = jnp.concatenate([jnp.where(lo_half, xp, zero), jnp.where(lo_half, zero, xp)], axis=0)
                res = _dot(lhs, rhs)
                state_ref[g, :, tile] = state_ref[g, :, tile] * ea_x[L - 1:L, tile] + res[L:, :]
                y = res[:L, :] + yacc_ref[g, :, tile] + xp.astype(F32) * dexp_ref[:, gtile]
                y = y * zs_ref[rows, gtile].astype(F32)
                yacc_ref[g, :, tile] = y
                ssq = ssq + y * y
            scale = lax.rsqrt(jnp.sum(ssq, axis=-1, keepdims=True) * (1.0 / gw) + EPS)
            o_ref[rows, gcols] = (yacc_ref[g] * scale * ng_ref[:, gcols]).astype(o_ref.dtype)
        return carry

    lax.fori_loop(0, zs_ref.shape[0] // L, chunk, 0, unroll=2)


def _ssd_branch(zs, xbc, dt_raw, dt_bias, a_log, d_skip, norm_g, bsz, seq):
    t, width = zs.shape
    conv_ch = xbc.shape[1]
    heads = dt_bias.shape[0]
    groups = SSD_GROUPS
    nstate = SSD_STATE
    L = SSD_CHUNK
    rows = _pick(seq, (8 * L, 4 * L, 2 * L, L))
    nb = seq // rows
    dtb = jnp.broadcast_to(dt_bias[:, None], (heads, L))
    alog = jnp.broadcast_to(a_log[:, None], (heads, L))
    dexp = jnp.repeat(d_skip, SSD_HEAD_DIM).reshape(1, width)
    expand = (jnp.arange(LANES)[:, None] == (jnp.arange(width) // SSD_HEAD_DIM)[None, :]).astype(BF16)
    row = lambda b, c: (b * nb + c, 0)
    const = lambda b, c: (0, 0)
    return pl.pallas_call(
        functools.partial(_ssd_kernel, width=width, groups=groups, nstate=nstate, heads=heads),
        grid=(bsz, nb),
        in_specs=[pl.BlockSpec((rows, width), row),
                  pl.BlockSpec((rows, conv_ch), row),
                  pl.BlockSpec((rows, LANES), row),
                  pl.BlockSpec((heads, L), const),
                  pl.BlockSpec((heads, L), const),
                  pl.BlockSpec((1, width), const),
                  pl.BlockSpec((1, width), const),
                  pl.BlockSpec((LANES, width), const)],
        out_specs=pl.BlockSpec((rows, width), row),
        out_shape=jax.ShapeDtypeStruct((t, width), BF16),
        scratch_shapes=[pltpu.VMEM((groups, nstate, width // groups), F32),
                        pltpu.VMEM((groups, L, width // groups), F32)],
        compiler_params=_cparams(("arbitrary", "arbitrary")),
        name="ssd",
    )(zs, xbc, dt_raw, dtb, alog, dexp, norm_g.reshape(1, width), expand)


def _lru_kernel(slg_ref, xl_ref, wg_ref, ba_ref, bx_ref, lam_ref, o_ref, a_ref, u_ref, carry_ref, *, rt):
    c = pl.program_id(1)
    rows, width = xl_ref.shape
    nt = width // LANES
    tg = width // LRU_TILE_GROUPS
    tiles_per_group = tg // LANES

    @pl.when(c == 0)
    def _():
        carry_ref[...] = jnp.zeros(carry_ref.shape, F32)

    def gate_tile(ti, carry):
        rs = pl.ds(pl.multiple_of(ti * rt, rt), rt)
        for g in range(LRU_TILE_GROUPS):
            gs = slice(g * tg, (g + 1) * tg)
            xl = xl_ref[rs, gs]
            gg = _dot(xl, wg_ref[g])
            r = _sigmoid_tanh(gg[:, :tg] + ba_ref[:, gs])
            i = _sigmoid_tanh(gg[:, tg:] + bx_ref[:, gs])
            a = jnp.exp2(r * ((-LRU_C * LOG2E) * _softplus(-lam_ref[:, gs])))
            u = _sqrt_nonneg(1.0 - a * a) * (i * xl.astype(F32))
            for k in range(tiles_per_group):
                a_ref[g * tiles_per_group + k, rs, :] = a[:, k * LANES:(k + 1) * LANES]
                u_ref[g * tiles_per_group + k, rs, :] = u[:, k * LANES:(k + 1) * LANES]
        return carry

    lax.fori_loop(0, rows // rt, gate_tile, 0, unroll=2)

    sub = lax.broadcasted_iota(jnp.int32, (SUBLANES, LANES), 0)

    def mini(m, carry):
        base = pl.multiple_of(m * SCAN_ROWS, SCAN_ROWS)
        for t in range(nt):
            cprev = carry_ref[:, t * LANES:(t + 1) * LANES]
            av = [a_ref[t, pl.ds(base + j, SUBLANES, stride=SCAN_STRIDE), :] for j in range(SCAN_STRIDE)]
            uv = [u_ref[t, pl.ds(base + j, SUBLANES, stride=SCAN_STRIDE), :] for j in range(SCAN_STRIDE)]
            h, p = uv[0], av[0]
            for j in range(1, SCAN_STRIDE):
                h = av[j] * h + uv[j]
                p = p * av[j]
            d = 1
            while d < SUBLANES:
                hs = pltpu.roll(h, d, 0)
                ps = pltpu.roll(p, d, 0)
                keep = sub < d
                h = jnp.where(keep, h, p * hs + h)
                p = jnp.where(keep, p, p * ps)
                d *= 2
            e = p * cprev + h
            hh = jnp.where(sub == 0, cprev, pltpu.roll(e, 1, 0))
            for j in range(SCAN_STRIDE):
                hh = av[j] * hh + uv[j]
                u_ref[t, pl.ds(base + j, SUBLANES, stride=SCAN_STRIDE), :] = hh
            carry_ref[:, t * LANES:(t + 1) * LANES] = jnp.broadcast_to(e[SUBLANES - 1:SUBLANES, :],
                                                                       (SUBLANES, LANES))
        return carry

    lax.fori_loop(0, rows // SCAN_ROWS, mini, 0)

    for t in range(nt):
        ts = slice(t * LANES, (t + 1) * LANES)
        o_ref[:, ts] = (u_ref[t] * slg_ref[:, ts].astype(F32)).astype(o_ref.dtype)


def _lru_branch(slg, xl, w_a, b_a, w_x, b_x, lam, bsz, seq):
    t, width = xl.shape
    rows = _pick(seq, (1024, 512, 256, 128))
    nb = seq // rows
    nblk, blk = w_a.shape[0], w_a.shape[1]
    per = nblk // LRU_TILE_GROUPS
    tg = width // LRU_TILE_GROUPS

    def tile_diag(w):
        wt = w.reshape(LRU_TILE_GROUPS, per, blk, blk)
        eye = jnp.eye(per, dtype=w.dtype)
        return jnp.einsum("gpij,pq->gpiqj", wt, eye).reshape(LRU_TILE_GROUPS, tg, tg)

    wg = jnp.concatenate([tile_diag(w_a), tile_diag(w_x)], axis=2).astype(BF16)
    row = lambda b, c: (b * nb + c, 0)
    const = lambda b, c: (0, 0)
    return pl.pallas_call(
        functools.partial(_lru_kernel, rt=min(rows, 128)),
        grid=(bsz, nb),
        in_specs=[pl.BlockSpec((rows, width), row),
                  pl.BlockSpec((rows, width), row),
                  pl.BlockSpec((LRU_TILE_GROUPS, tg, 2 * tg), lambda b, c: (0, 0, 0)),
                  pl.BlockSpec((1, width), const),
                  pl.BlockSpec((1, width), const),
                  pl.BlockSpec((1, width), const)],
        out_specs=pl.BlockSpec((rows, width), row),
        out_shape=jax.ShapeDtypeStruct((t, width), BF16),
        scratch_shapes=[pltpu.VMEM((width // LANES, rows, LANES), F32),
                        pltpu.VMEM((width // LANES, rows, LANES), F32),
                        pltpu.VMEM((SUBLANES, width), F32)],
        compiler_params=_cparams(("arbitrary", "arbitrary")),
        name="lru",
    )(slg, xl, wg, b_a.reshape(1, width), b_x.reshape(1, width), lam.reshape(1, width))


def _kv_kernel(m_ref, g_ref, w_ref, o_ref):
    m = m_ref[...]
    ms = jnp.mean(m * m, axis=-1, keepdims=True)
    mn = (m * lax.rsqrt(ms + EPS) * g_ref[...]).astype(BF16)
    o_ref[...] = _dot(mn, w_ref[...]).astype(o_ref.dtype)


def _mem_kv(mem2d, g, w_kv_b, tn):
    m, d = mem2d.shape
    n = w_kv_b.shape[1]
    return pl.pallas_call(
        _kv_kernel,
        grid=(n // tn,),
        in_specs=[pl.BlockSpec((m, d), lambda j: (0, 0)),
                  pl.BlockSpec((1, d), lambda j: (0, 0)),
                  pl.BlockSpec((d, tn), lambda j: (0, j))],
        out_specs=pl.BlockSpec((m, tn), lambda j: (0, j)),
        out_shape=jax.ShapeDtypeStruct((m, n), BF16),
        compiler_params=_cparams(("parallel",)),
        name="mem_kv",
    )(mem2d, g.reshape(1, d), w_kv_b)


def _attn_kernel(q_ref, kv_ref, o_ref):
    d = q_ref.shape[1]
    hd = d // MEM_HEADS
    scale = hd ** -0.5
    sub = min(q_ref.shape[0], 256)

    def block(bi, carry):
        rows = pl.ds(pl.multiple_of(bi * sub, sub), sub)
        for h in range(MEM_HEADS):
            hs = slice(h * hd, (h + 1) * hd)
            s = _dot_nt(q_ref[rows, hs], kv_ref[:, hs]) * scale
            p = jnp.exp(s - jnp.max(s, axis=-1, keepdims=True))
            l = jnp.sum(p, axis=-1, keepdims=True)
            o = _dot(p.astype(BF16), kv_ref[:, d + h * hd:d + (h + 1) * hd]) / l
            o_ref[rows, hs] = o.astype(o_ref.dtype)
        return carry

    lax.fori_loop(0, q_ref.shape[0] // sub, block, 0, unroll=2)


def _mem_attention(q, kv, bsz, seq, tq):
    t, d = q.shape
    m = kv.shape[0] // bsz
    nq = seq // tq
    return pl.pallas_call(
        _attn_kernel,
        grid=(bsz, nq),
        in_specs=[pl.BlockSpec((tq, d), lambda b, i: (b * nq + i, 0)),
                  pl.BlockSpec((m, 2 * d), lambda b, i: (b, 0))],
        out_specs=pl.BlockSpec((tq, d), lambda b, i: (b * nq + i, 0)),
        out_shape=jax.ShapeDtypeStruct((t, d), BF16),
        compiler_params=_cparams(("parallel", "parallel")),
        name="mem_attn",
    )(q, kv)


def _merge_kernel(ys_ref, yl_ref, ym_ref, gt_ref, x_ref, ws_ref, wl_ref, wm_ref, wo_ref, fg_ref, o_ref, *,
                  final_norm):
    d = x_ref.shape[1]
    merged = (gt_ref[:, :d].astype(F32) * _dot(ys_ref[...], ws_ref[...])
              + gt_ref[:, d:2 * d].astype(F32) * _dot(yl_ref[...], wl_ref[...])
              + gt_ref[:, 2 * d:].astype(F32) * _dot(ym_ref[...], wm_ref[...]))
    xn = x_ref[...] + _dot(merged.astype(BF16), wo_ref[...])
    if final_norm:
        ms = jnp.mean(xn * xn, axis=-1, keepdims=True)
        xn = xn * lax.rsqrt(ms + EPS) * fg_ref[...]
    o_ref[...] = xn


def _merge(y_ssd, y_lru, y_mem, gates, x2d, w_s, w_l, w_m, w_o, final_g, final_norm, tm):
    t, d = x2d.shape
    row = lambda i: (i, 0)
    const = lambda i: (0, 0)
    full = lambda a: pl.BlockSpec(a.shape, const)
    return pl.pallas_call(
        functools.partial(_merge_kernel, final_norm=final_norm),
        grid=(t // tm,),
        in_specs=[pl.BlockSpec((tm, y_ssd.shape[1]), row),
                  pl.BlockSpec((tm, y_lru.shape[1]), row),
                  pl.BlockSpec((tm, y_mem.shape[1]), row),
                  pl.BlockSpec((tm, gates.shape[1]), row),
                  pl.BlockSpec((tm, d), row),
                  full(w_s), full(w_l), full(w_m), full(w_o),
                  pl.BlockSpec((1, d), const)],
        out_specs=pl.BlockSpec((tm, d), row),
        out_shape=jax.ShapeDtypeStruct((t, d), F32),
        compiler_params=_cparams(("parallel",)),
        name="merge",
    )(y_ssd, y_lru, y_mem, gates, x2d, w_s, w_l, w_m, w_o, final_g.reshape(1, d))


def kernel(x, mem, norm_g, w_in, ssd_conv_w, ssd_conv_b, ssd_dt_bias, ssd_a_log, ssd_d, ssd_norm_g, lru_conv_w, lru_conv_b, lru_w_a, lru_b_a, lru_w_x, lru_b_x, lru_lambda, mem_norm_g, w_kv, w_br_ssd, w_br_lru, w_br_mem, w_out, final_g):
    bsz, seq, d = x.shape
    depth = norm_g.shape[0]
    t = bsz * seq
    heads = ssd_dt_bias.shape[1]
    ssd_w = heads * SSD_HEAD_DIM
    conv_ch = ssd_conv_w.shape[2]
    lru_w = lru_conv_w.shape[2]
    tm = _pick(t, (512, 256, 128))
    x2d = x.reshape(t, d)
    mem2d = mem.reshape(bsz * mem.shape[1], d)
    front = ssd_w + conv_ch
    tn = lambda n: _pick(n, (1024, 768, 512, 256, 128))
    for l in range(depth):
        wt = jnp.swapaxes(w_in[l], 0, 1)
        wt_dt = jnp.pad(wt[front:front + heads], ((0, LANES - heads), (0, 0)))
        wt_rest = wt[front + heads:]

        h, q, dt_raw = _norm_q_dt(x2d, norm_g[l], wt_rest, 2 * lru_w, d, wt_dt, seq)
        zs = _proj(h, wt, 0, ssd_w, tn(ssd_w), seq, "proj_z", act="silu")
        xbc = _proj(h, wt, ssd_w, conv_ch, tn(conv_ch), seq, "proj_xbc", act="silu",
                    conv_w=ssd_conv_w[l], conv_b=ssd_conv_b[l])
        slg = _proj(h, wt_rest, 0, lru_w, tn(lru_w), seq, "proj_lg", act="silu")
        xl = _proj(h, wt_rest, lru_w, lru_w, tn(lru_w), seq, "proj_lx",
                   conv_w=lru_conv_w[l], conv_b=lru_conv_b[l])
        gates = _proj(h, wt_rest, 2 * lru_w + d, 3 * d, tn(3 * d), seq, "proj_gates", act="sigmoid")

        y_ssd = _ssd_branch(zs, xbc, dt_raw, ssd_dt_bias[l], ssd_a_log[l], ssd_d[l],
                            ssd_norm_g[l].reshape(-1), bsz, seq)
        y_lru = _lru_branch(slg, xl, lru_w_a[l], lru_b_a[l].reshape(-1), lru_w_x[l], lru_b_x[l].reshape(-1),
                            lru_lambda[l], bsz, seq)
        kv = _mem_kv(mem2d, mem_norm_g[l], w_kv[l].astype(BF16), _pick(2 * d, (512, 256, 128)))
        y_mem = _mem_attention(q, kv, bsz, seq, _pick(seq, (2048, 1024, 512, 256, 128)))
        x2d = _merge(y_ssd, y_lru, y_mem, gates, x2d, w_br_ssd[l].astype(BF16), w_br_lru[l].astype(BF16),
                     w_br_mem[l].astype(BF16), w_out[l].astype(BF16), final_g, l == depth - 1, tm)
    return x2d.reshape(bsz, seq, d)
```

```python
import functools

import jax
import jax.numpy as jnp
from jax import lax
from jax.experimental import pallas as pl
from jax.experimental.pallas import tpu as pltpu

F32 = jnp.float32
BF16 = jnp.bfloat16

EPS = 1e-6
CONV_WIDTH = 4
SSD_HEAD_DIM = 64
SSD_GROUPS = 4
SSD_STATE = 128
SSD_CHUNK = 128
LRU_C = 8.0
MEM_HEADS = 4
LANES = 128
SUBLANES = 8
LRU_TILE_GROUPS = 4
SCAN_STRIDE = 4
SCAN_ROWS = SCAN_STRIDE * SUBLANES

VMEM_LIMIT = 56 * 1024 * 1024


def _cparams(sem):
    return pltpu.CompilerParams(dimension_semantics=sem, vmem_limit_bytes=VMEM_LIMIT)


LOG2E = 1.4426950408889634
TINY = 1e-30


def _sigmoid(x):
    return 1.0 / (1.0 + jnp.exp2(x * (-LOG2E)))


def _sigmoid_tanh(x):
    return 0.5 * jnp.tanh(0.5 * x) + 0.5


def _sqrt_nonneg(x):
    return x * lax.rsqrt(jnp.maximum(x, TINY))


def _silu(x):
    return x * _sigmoid(x)


def _softplus(x):
    return jnp.maximum(x, 0.0) + jnp.log1p(jnp.exp(-jnp.abs(x)))


def _split2(x):
    hi = x.astype(BF16)
    lo = (x - hi.astype(F32)).astype(BF16)
    return hi, lo


def _split3(x):
    hi = x.astype(BF16)
    r = x - hi.astype(F32)
    mid = r.astype(BF16)
    lo = (r - mid.astype(F32)).astype(BF16)
    return hi, mid, lo


def _dot(a, b):
    return jnp.dot(a, b, preferred_element_type=F32)


def _dot_nt(a, b):
    return lax.dot_general(a, b, (((1,), (1,)), ((), ())), preferred_element_type=F32)


def _pick(n, prefs):
    for p in prefs:
        if n % p == 0:
            return p
    return n


def _proj_kernel(*refs, sub, act, conv, blocks_per_seq):
    if conv:
        h_ref, wt_ref, cw_ref, cb_ref, o_ref, w_ref, pad_ref = refs
    else:
        h_ref, wt_ref, o_ref, w_ref, acc0_ref, acc1_ref = refs
        acc_refs = (acc0_ref, acc1_ref)
    tm = h_ref.shape[0]
    nsub = tm // sub
    i = pl.program_id(1)

    @pl.when(i == 0)
    def _():
        w_ref[...] = wt_ref[...].T.astype(BF16)

    if conv:
        @pl.when(i % blocks_per_seq == 0)
        def _():
            pad_ref[:, 0:2 * SUBLANES, :] = jnp.zeros((pad_ref.shape[0], 2 * SUBLANES, LANES), F32)

        for r in range(nsub):
            rows = slice(r * sub, (r + 1) * sub)
            acc = _dot(h_ref[rows, :], w_ref[...])
            for c in range(pad_ref.shape[0]):
                cols = slice(c * LANES, (c + 1) * LANES)
                x = acc[:, cols]
                pad_ref[c, pl.ds(2 * SUBLANES, sub, stride=2), :] = x
                y = cb_ref[:, cols] + cw_ref[CONV_WIDTH - 1:CONV_WIDTH, cols] * x
                for s in range(1, CONV_WIDTH):
                    y = y + (cw_ref[CONV_WIDTH - 1 - s:CONV_WIDTH - s, cols]
                             * pad_ref[c, pl.ds(2 * (SUBLANES - s), sub, stride=2), :])
                pad_ref[c, pl.ds(0, SUBLANES, stride=2), :] = x[sub - SUBLANES:, :]
                o_ref[rows, cols] = (_silu(y) if act == "silu" else y).astype(o_ref.dtype)
        return

    for r in range(nsub + 1):
        if r > 0:
            acc = acc_refs[(r - 1) % 2][...]
            o_ref[(r - 1) * sub:r * sub, :] = (_silu(acc) if act == "silu" else _sigmoid(acc)).astype(o_ref.dtype)
        if r < nsub:
            acc_refs[r % 2][...] = _dot(h_ref[r * sub:(r + 1) * sub, :], w_ref[...])


def _proj(h, wt, row0, ncols, tn, seq, name, act=None, conv_w=None, conv_b=None, out_dtype=BF16):
    t, k = h.shape
    tm = _pick(seq, (2048, 1024, 512, 256, 128))
    conv = conv_w is not None
    sub = min(tm, 512 if conv else 256)
    in_specs = [pl.BlockSpec((tm, k), lambda j, i: (i, 0)),
                pl.BlockSpec((pl.Element(tn), pl.Element(k)),
                             lambda j, i: (pl.multiple_of(row0 + j * tn, SUBLANES), 0))]
    args = [h, wt]
    scratch = [pltpu.VMEM((k, tn), BF16)]
    if conv:
        in_specs += [pl.BlockSpec((CONV_WIDTH, tn), lambda j, i: (0, j)),
                     pl.BlockSpec((1, tn), lambda j, i: (0, j))]
        args += [conv_w, conv_b.reshape(1, ncols)]
        scratch += [pltpu.VMEM((tn // LANES, 2 * (sub + SUBLANES), LANES), F32)]
    else:
        scratch += [pltpu.VMEM((sub, tn), F32), pltpu.VMEM((sub, tn), F32)]
    return pl.pallas_call(
        functools.partial(_proj_kernel, sub=sub, act=act, conv=conv, blocks_per_seq=seq // tm),
        grid=(ncols // tn, t // tm),
        in_specs=in_specs,
        out_specs=pl.BlockSpec((tm, tn), lambda j, i: (i, j)),
        out_shape=jax.ShapeDtypeStruct((t, ncols), out_dtype),
        scratch_shapes=scratch,
        compiler_params=_cparams(("arbitrary", "arbitrary")),
        name=name,
    )(*args)


def _norm_q_dt_kernel(x_ref, g_ref, wqt_ref, wdtt_ref, h_ref, q_ref, dt_ref, wq_ref, wdt_ref, *, sub):
    tm = x_ref.shape[0]

    @pl.when(pl.program_id(0) == 0)
    def _():
        wq_ref[...] = wqt_ref[...].T.astype(BF16)
        wdt_ref[...] = wdtt_ref[...].T.astype(BF16)

    for r in range(tm // sub):
        rows = slice(r * sub, (r + 1) * sub)
        x = x_ref[rows, :]
        ms = jnp.mean(x * x, axis=-1, keepdims=True)
        h = (x * lax.rsqrt(ms + EPS) * g_ref[...]).astype(BF16)
        h_ref[rows, :] = h
        q_ref[rows, :] = _dot(h, wq_ref[...]).astype(q_ref.dtype)
        dt_ref[rows, :] = _dot(h, wdt_ref[...])


def _norm_q_dt(x2d, g, wt, row0, ncols, dt_row0, seq):
    t, k = x2d.shape
    tm = _pick(seq, (1024, 512, 256, 128))
    sub = min(tm, 256)
    return pl.pallas_call(
        functools.partial(_norm_q_dt_kernel, sub=sub),
        grid=(t // tm,),
        in_specs=[pl.BlockSpec((tm, k), lambda i: (i, 0)),
                  pl.BlockSpec((1, k), lambda i: (0, 0)),
                  pl.BlockSpec((pl.Element(ncols), pl.Element(k)), lambda i: (row0, 0)),
                  pl.BlockSpec((pl.Element(LANES), pl.Element(k)), lambda i: (dt_row0, 0))],
        out_specs=[pl.BlockSpec((tm, k), lambda i: (i, 0)),
                   pl.BlockSpec((tm, ncols), lambda i: (i, 0)),
                   pl.BlockSpec((tm, LANES), lambda i: (i, 0))],
        out_shape=[jax.ShapeDtypeStruct((t, k), BF16),
                   jax.ShapeDtypeStruct((t, ncols), BF16),
                   jax.ShapeDtypeStruct((t, LANES), F32)],
        scratch_shapes=[pltpu.VMEM((k, ncols), BF16), pltpu.VMEM((k, LANES), BF16)],
        compiler_params=_cparams(("arbitrary",)),
        name="norm_q_dt",
    )(x2d, g.reshape(1, k), wt, wt)


def _ssd_kernel(zs_ref, xbc_ref, dt_ref, dtb_ref, alog_ref, dexp_ref, ng_ref, e_ref,
                o_ref, state_ref, yacc_ref, rowterm_ref, wgt_ref, acs_ref, eahi_ref, ealo_ref,
                *, width, groups, nstate, heads):
    c = pl.program_id(1)
    L = SSD_CHUNK
    gw = width // groups
    pairs = gw // LANES
    b_off = width
    c_off = width + groups * nstate

    @pl.when(c == 0)
    def _():
        state_ref[...] = jnp.zeros(state_ref.shape, F32)

    row = lax.broadcasted_iota(jnp.int32, (L, L), 0)
    col = lax.broadcasted_iota(jnp.int32, (L, L), 1)
    causal = row >= col
    triu = (row <= col).astype(BF16)
    lo_half = lax.broadcasted_iota(jnp.int32, (L, LANES), 1) < SSD_HEAD_DIM

    nchunks = zs_ref.shape[0] // L
    a_neg = -jnp.exp(alog_ref[...])
    dt_ts = [_softplus(dt_ref[k * L:(k + 1) * L, :].T[:heads, :] + dtb_ref[...]) for k in range(nchunks)]
    parts = [_split3(dt_t * a_neg) for dt_t in dt_ts]
    acs_ts = [sum(_dot(part, triu) for part in p) for p in parts]
    for k in range(nchunks):
        acs_t, dt_t = acs_ts[k], dt_ts[k]
        rowterm_ref[k] = (acs_t - jnp.log(dt_t)) * LOG2E
        wgt_ref[k] = dt_t * jnp.exp(acs_t[:, L - 1:L] - acs_t)
        acs_n = jnp.concatenate([acs_t * LOG2E, jnp.zeros((LANES - heads, L), F32)], axis=0).T
        acs_ref[k] = acs_n
        ea_hi, ea_lo = _split2(jnp.exp2(acs_n))
        eahi_ref[k] = ea_hi
        ealo_ref[k] = ea_lo

    def chunk(ci, carry):
        rows = pl.ds(pl.multiple_of(ci * L, L), L)
        rowterm_t = rowterm_ref[ci]
        wgt_t = wgt_ref[ci]
        acs_n = acs_ref[ci]
        ea_hi = eahi_ref[ci]
        ea_lo = ealo_ref[ci]

        for g in range(groups):
            gcols = slice(g * gw, (g + 1) * gw)
            cg = xbc_ref[rows, c_off + g * nstate:c_off + (g + 1) * nstate]
            bg = xbc_ref[rows, b_off + g * nstate:b_off + (g + 1) * nstate]
            cb = _dot_nt(cg, bg)
            bgt = bg.astype(F32).T
            ea_x = _dot(ea_hi, e_ref[:, gcols]) + _dot(ea_lo, e_ref[:, gcols])
            yacc_ref[g] = _dot(cg, state_ref[g].astype(BF16)) * ea_x
            ssq = jnp.zeros((L, LANES), F32)
            for j in range(pairs):
                h0 = (g * pairs + j) * 2
                tile = slice(j * LANES, (j + 1) * LANES)
                gtile = slice(g * gw + j * LANES, g * gw + (j + 1) * LANES)
                top, bot = [], []
                for h in (h0, h0 + 1):
                    seg = acs_n[:, h:h + 1] - rowterm_t[h:h + 1, :]
                    top.append(jnp.exp2(jnp.where(causal, seg, -jnp.inf)) * cb)
                    bot.append(bgt * wgt_t[h:h + 1, :])
                lhs = jnp.concatenate([jnp.concatenate(top, axis=1), jnp.concatenate(bot, axis=1)],
                                      axis=0).astype(BF16)
                xp = xbc_ref[rows, gtile]
                zero = jnp.zeros_like(xp)
                rhs = jnp.concatenate([jnp.where(lo_half, xp, zero), jnp.where(lo_half, zero, xp)], axis=0)
                res = _dot(lhs, rhs)
                state_ref[g, :, tile] = state_ref[g, :, tile] * ea_x[L - 1:L, tile] + res[L:, :]
                y = res[:L, :] + yacc_ref[g, :, tile] + xp.astype(F32) * dexp_ref[:, gtile]
                y = y * zs_ref[rows, gtile].astype(F32)
                yacc_ref[g, :, tile] = y
                ssq = ssq + y * y
            scale = lax.rsqrt(jnp.sum(ssq, axis=-1, keepdims=True) * (1.0 / gw) + EPS)
            o_ref[rows, gcols] = (yacc_ref[g] * scale * ng_ref[:, gcols]).astype(o_ref.dtype)
        return carry

    lax.fori_loop(0, zs_ref.shape[0] // L, chunk, 0, unroll=2)


def _ssd_branch(zs, xbc, dt_raw, dt_bias, a_log, d_skip, norm_g, bsz, seq):
    t, width = zs.shape
    conv_ch = xbc.shape[1]
    heads = dt_bias.shape[0]
    groups = SSD_GROUPS
    nstate = SSD_STATE
    L = SSD_CHUNK
    rows = _pick(seq, (8 * L, 4 * L, 2 * L, L))
    nb = seq // rows
    dtb = jnp.broadcast_to(dt_bias[:, None], (heads, L))
    alog = jnp.broadcast_to(a_log[:, None], (heads, L))
    dexp = jnp.repeat(d_skip, SSD_HEAD_DIM).reshape(1, width)
    expand = (jnp.arange(LANES)[:, None] == (jnp.arange(width) // SSD_HEAD_DIM)[None, :]).astype(BF16)
    row = lambda b, c: (b * nb + c, 0)
    const = lambda b, c: (0, 0)
    return pl.pallas_call(
        functools.partial(_ssd_kernel, width=width, groups=groups, nstate=nstate, heads=heads),
        grid=(bsz, nb),
        in_specs=[pl.BlockSpec((rows, width), row),
                  pl.BlockSpec((rows, conv_ch), row),
                  pl.BlockSpec((rows, LANES), row),
                  pl.BlockSpec((heads, L), const),
                  pl.BlockSpec((heads, L), const),
                  pl.BlockSpec((1, width), const),
                  pl.BlockSpec((1, width), const),
                  pl.BlockSpec((LANES, width), const)],
        out_specs=pl.BlockSpec((rows, width), row),
        out_shape=jax.ShapeDtypeStruct((t, width), BF16),
        scratch_shapes=[pltpu.VMEM((groups, nstate, width // groups), F32),
                        pltpu.VMEM((groups, L, width // groups), F32),
                        pltpu.VMEM((rows // L, heads, L), F32),
                        pltpu.VMEM((rows // L, heads, L), F32),
                        pltpu.VMEM((rows // L, L, LANES), F32),
                        pltpu.VMEM((rows // L, L, LANES), BF16),
                        pltpu.VMEM((rows // L, L, LANES), BF16)],
        compiler_params=_cparams(("arbitrary", "arbitrary")),
        name="ssd",
    )(zs, xbc, dt_raw, dtb, alog, dexp, norm_g.reshape(1, width), expand)


def _lru_kernel(slg_ref, xl_ref, wg_ref, ba_ref, bx_ref, lam_ref, o_ref, a_ref, u_ref, carry_ref, *, rt):
    c = pl.program_id(1)
    rows, width = xl_ref.shape
    nt = width // LANES
    tg = width // LRU_TILE_GROUPS
    tiles_per_group = tg // LANES

    @pl.when(c == 0)
    def _():
        carry_ref[...] = jnp.zeros(carry_ref.shape, F32)

    def gate_tile(ti, carry):
        rs = pl.ds(pl.multiple_of(ti * rt, rt), rt)
        for g in range(LRU_TILE_GROUPS):
            gs = slice(g * tg, (g + 1) * tg)
            xl = xl_ref[rs, gs]
            gg = _dot(xl, wg_ref[g])
            r = _sigmoid_tanh(gg[:, :tg] + ba_ref[:, gs])
            i = _sigmoid_tanh(gg[:, tg:] + bx_ref[:, gs])
            a = jnp.exp2(r * ((-LRU_C * LOG2E) * _softplus(-lam_ref[:, gs])))
            u = _sqrt_nonneg(1.0 - a * a) * (i * xl.astype(F32))
            for k in range(tiles_per_group):
                a_ref[g * tiles_per_group + k, rs, :] = a[:, k * LANES:(k + 1) * LANES]
                u_ref[g * tiles_per_group + k, rs, :] = u[:, k * LANES:(k + 1) * LANES]
        return carry

    lax.fori_loop(0, rows // rt, gate_tile, 0, unroll=2)

    sub = lax.broadcasted_iota(jnp.int32, (SUBLANES, LANES), 0)

    def mini(m, carry):
        base = pl.multiple_of(m * SCAN_ROWS, SCAN_ROWS)
        for t in range(nt):
            cprev = carry_ref[:, t * LANES:(t + 1) * LANES]
            av = [a_ref[t, pl.ds(base + j, SUBLANES, stride=SCAN_STRIDE), :] for j in range(SCAN_STRIDE)]
            uv = [u_ref[t, pl.ds(base + j, SUBLANES, stride=SCAN_STRIDE), :] for j in range(SCAN_STRIDE)]
            h, p = uv[0], av[0]
            for j in range(1, SCAN_STRIDE):
                h = av[j] * h + uv[j]
                p = p * av[j]
            d = 1
            while d < SUBLANES:
                hs = pltpu.roll(h, d, 0)
                ps = pltpu.roll(p, d, 0)
                keep = sub < d
                h = jnp.where(keep, h, p * hs + h)
                p = jnp.where(keep, p, p * ps)
                d *= 2
            e = p * cprev + h
            hh = jnp.where(sub == 0, cprev, pltpu.roll(e, 1, 0))
            for j in range(SCAN_STRIDE):
                hh = av[j] * hh + uv[j]
                u_ref[t, pl.ds(base + j, SUBLANES, stride=SCAN_STRIDE), :] = hh
            carry_ref[:, t * LANES:(t + 1) * LANES] = jnp.broadcast_to(e[SUBLANES - 1:SUBLANES, :],
                                                                       (SUBLANES, LANES))
        return carry

    lax.fori_loop(0, rows // SCAN_ROWS, mini, 0)

    for t in range(nt):
        ts = slice(t * LANES, (t + 1) * LANES)
        o_ref[:, ts] = (u_ref[t] * slg_ref[:, ts].astype(F32)).astype(o_ref.dtype)


def _lru_branch(slg, xl, w_a, b_a, w_x, b_x, lam, bsz, seq):
    t, width = xl.shape
    rows = _pick(seq, (1024, 512, 256, 128))
    nb = seq // rows
    nblk, blk = w_a.shape[0], w_a.shape[1]
    per = nblk // LRU_TILE_GROUPS
    tg = width // LRU_TILE_GROUPS

    def tile_diag(w):
        wt = w.reshape(LRU_TILE_GROUPS, per, blk, blk)
        eye = jnp.eye(per, dtype=w.dtype)
        return jnp.einsum("gpij,pq->gpiqj", wt, eye).reshape(LRU_TILE_GROUPS, tg, tg)

    wg = jnp.concatenate([tile_diag(w_a), tile_diag(w_x)], axis=2).astype(BF16)
    row = lambda b, c: (b * nb + c, 0)
    const = lambda b, c: (0, 0)
    return pl.pallas_call(
        functools.partial(_lru_kernel, rt=min(rows, 128)),
        grid=(bsz, nb),
        in_specs=[pl.BlockSpec((rows, width), row),
                  pl.BlockSpec((rows, width), row),
                  pl.BlockSpec((LRU_TILE_GROUPS, tg, 2 * tg), lambda b, c: (0, 0, 0)),
                  pl.BlockSpec((1, width), const),
                  pl.BlockSpec((1, width), const),
                  pl.BlockSpec((1, width), const)],
        out_specs=pl.BlockSpec((rows, width), row),
        out_shape=jax.ShapeDtypeStruct((t, width), BF16),
        scratch_shapes=[pltpu.VMEM((width // LANES, rows, LANES), F32),
                        pltpu.VMEM((width // LANES, rows, LANES), F32),
                        pltpu.VMEM((SUBLANES, width), F32)],
        compiler_params=_cparams(("arbitrary", "arbitrary")),
        name="lru",
    )(slg, xl, wg, b_a.reshape(1, width), b_x.reshape(1, width), lam.reshape(1, width))


def _kv_kernel(m_ref, g_ref, w_ref, o_ref):
    m = m_ref[...]
    ms = jnp.mean(m * m, axis=-1, keepdims=True)
    mn = (m * lax.rsqrt(ms + EPS) * g_ref[...]).astype(BF16)
    o_ref[...] = _dot(mn, w_ref[...]).astype(o_ref.dtype)


def _mem_kv(mem2d, g, w_kv_b, tn):
    m, d = mem2d.shape
    n = w_kv_b.shape[1]
    return pl.pallas_call(
        _kv_kernel,
        grid=(n // tn,),
        in_specs=[pl.BlockSpec((m, d), lambda j: (0, 0)),
                  pl.BlockSpec((1, d), lambda j: (0, 0)),
                  pl.BlockSpec((d, tn), lambda j: (0, j))],
        out_specs=pl.BlockSpec((m, tn), lambda j: (0, j)),
        out_shape=jax.ShapeDtypeStruct((m, n), BF16),
        compiler_params=_cparams(("parallel",)),
        name="mem_kv",
    )(mem2d, g.reshape(1, d), w_kv_b)


def _merge_kernel(ys_ref, yl_ref, q_ref, kv_ref, gt_ref, x_ref, ws_ref, wl_ref, wm_ref, wo_ref, fg_ref, o_ref,
                  ym_ref, *, final_norm):
    d = x_ref.shape[1]
    hd = d // MEM_HEADS
    scale = hd ** -0.5
    sub = min(q_ref.shape[0], 256)
    for bi in range(q_ref.shape[0] // sub):
        rows = slice(bi * sub, (bi + 1) * sub)
        for h in range(MEM_HEADS):
            hs = slice(h * hd, (h + 1) * hd)
            s = _dot_nt(q_ref[rows, hs], kv_ref[:, hs]) * scale
            p = jnp.exp(s - jnp.max(s, axis=-1, keepdims=True))
            l = jnp.sum(p, axis=-1, keepdims=True)
            o = _dot(p.astype(BF16), kv_ref[:, d + h * hd:d + (h + 1) * hd]) / l
            ym_ref[rows, hs] = o.astype(ym_ref.dtype)
    merged = (gt_ref[:, :d].astype(F32) * _dot(ys_ref[...], ws_ref[...])
              + gt_ref[:, d:2 * d].astype(F32) * _dot(yl_ref[...], wl_ref[...])
              + gt_ref[:, 2 * d:].astype(F32) * _dot(ym_ref[...], wm_ref[...]))
    xn = x_ref[...] + _dot(merged.astype(BF16), wo_ref[...])
    if final_norm:
        ms = jnp.mean(xn * xn, axis=-1, keepdims=True)
        xn = xn * lax.rsqrt(ms + EPS) * fg_ref[...]
    o_ref[...] = xn


def _merge(y_ssd, y_lru, q, kv, gates, x2d, w_s, w_l, w_m, w_o, final_g, final_norm, seq):
    t, d = x2d.shape
    tm = _pick(seq, (512, 256, 128))
    blocks_per_seq = seq // tm
    m = kv.shape[0] // (t // seq)
    row = lambda i: (i, 0)
    const = lambda i: (0, 0)
    full = lambda a: pl.BlockSpec(a.shape, const)
    return pl.pallas_call(
        functools.partial(_merge_kernel, final_norm=final_norm),
        grid=(t // tm,),
        in_specs=[pl.BlockSpec((tm, y_ssd.shape[1]), row),
                  pl.BlockSpec((tm, y_lru.shape[1]), row),
                  pl.BlockSpec((tm, d), row),
                  pl.BlockSpec((m, 2 * d), lambda i: (i // blocks_per_seq, 0)),
                  pl.BlockSpec((tm, gates.shape[1]), row),
                  pl.BlockSpec((tm, d), row),
                  full(w_s), full(w_l), full(w_m), full(w_o),
                  pl.BlockSpec((1, d), const)],
        out_specs=pl.BlockSpec((tm, d), row),
        out_shape=jax.ShapeDtypeStruct((t, d), F32),
        scratch_shapes=[pltpu.VMEM((tm, d), BF16)],
        compiler_params=_cparams(("parallel",)),
        name="merge",
    )(y_ssd, y_lru, q, kv, gates, x2d, w_s, w_l, w_m, w_o, final_g.reshape(1, d))


def kernel(x, mem, norm_g, w_in, ssd_conv_w, ssd_conv_b, ssd_dt_bias, ssd_a_log, ssd_d, ssd_norm_g, lru_conv_w, lru_conv_b, lru_w_a, lru_b_a, lru_w_x, lru_b_x, lru_lambda, mem_norm_g, w_kv, w_br_ssd, w_br_lru, w_br_mem, w_out, final_g):
    bsz, seq, d = x.shape
    depth = norm_g.shape[0]
    t = bsz * seq
    heads = ssd_dt_bias.shape[1]
    ssd_w = heads * SSD_HEAD_DIM
    conv_ch = ssd_conv_w.shape[2]
    lru_w = lru_conv_w.shape[2]
    x2d = x.reshape(t, d)
    mem2d = mem.reshape(bsz * mem.shape[1], d)
    front = ssd_w + conv_ch
    tn = lambda n: _pick(n, (1024, 768, 512, 256, 128))
    for l in range(depth):
        wt = jnp.swapaxes(w_in[l], 0, 1)
        rest = front + heads

        h, q, dt_raw = _norm_q_dt(x2d, norm_g[l], wt, rest + 2 * lru_w, d, front, seq)
        zs = _proj(h, wt, 0, ssd_w, tn(ssd_w), seq, "proj_z", act="silu")
        xbc = _proj(h, wt, ssd_w, conv_ch, tn(conv_ch), seq, "proj_xbc", act="silu",
                    conv_w=ssd_conv_w[l], conv_b=ssd_conv_b[l])
        slg = _proj(h, wt, rest, lru_w, tn(lru_w), seq, "proj_lg", act="silu")
        xl = _proj(h, wt, rest + lru_w, lru_w, tn(lru_w), seq, "proj_lx",
                   conv_w=lru_conv_w[l], conv_b=lru_conv_b[l])
        gates = _proj(h, wt, rest + 2 * lru_w + d, 3 * d, tn(3 * d), seq, "proj_gates", act="sigmoid")

        y_ssd = _ssd_branch(zs, xbc, dt_raw, ssd_dt_bias[l], ssd_a_log[l], ssd_d[l],
                            ssd_norm_g[l].reshape(-1), bsz, seq)
        y_lru = _lru_branch(slg, xl, lru_w_a[l], lru_b_a[l].reshape(-1), lru_w_x[l], lru_b_x[l].reshape(-1),
                            lru_lambda[l], bsz, seq)
        kv = _mem_kv(mem2d, mem_norm_g[l], w_kv[l].astype(BF16), _pick(2 * d, (512, 256, 128)))
        x2d = _merge(y_ssd, y_lru, q, kv, gates, x2d, w_br_ssd[l].astype(BF16), w_br_lru[l].astype(BF16),
                     w_br_mem[l].astype(BF16), w_out[l].astype(BF16), final_g, l == depth - 1, seq)
    return x2d.reshape(bsz, seq, d)
```

```python
import functools

import jax
import jax.numpy as jnp
from jax import lax
from jax.experimental import pallas as pl
from jax.experimental.pallas import tpu as pltpu

F32 = jnp.float32
BF16 = jnp.bfloat16

EPS = 1e-6
CONV_WIDTH = 4
SSD_HEAD_DIM = 64
SSD_GROUPS = 4
SSD_STATE = 128
SSD_CHUNK = 128
LRU_C = 8.0
MEM_HEADS = 4
LANES = 128
SUBLANES = 8
LRU_TILE_GROUPS = 4
SCAN_STRIDE = 4
SCAN_ROWS = SCAN_STRIDE * SUBLANES

VMEM_LIMIT = 56 * 1024 * 1024


def _cparams(sem):
    return pltpu.CompilerParams(dimension_semantics=sem, vmem_limit_bytes=VMEM_LIMIT)


LOG2E = 1.4426950408889634
TINY = 1e-30


def _sigmoid(x):
    return 1.0 / (1.0 + jnp.exp2(x * (-LOG2E)))


def _sqrt_nonneg(x):
    return x * lax.rsqrt(jnp.maximum(x, TINY))


def _silu(x):
    return x * _sigmoid(x)


def _softplus(x):
    return jnp.maximum(x, 0.0) + jnp.log1p(jnp.exp(-jnp.abs(x)))


def _split2(x):
    hi = x.astype(BF16)
    lo = (x - hi.astype(F32)).astype(BF16)
    return hi, lo


def _split3(x):
    hi = x.astype(BF16)
    r = x - hi.astype(F32)
    mid = r.astype(BF16)
    lo = (r - mid.astype(F32)).astype(BF16)
    return hi, mid, lo


def _dot(a, b):
    return jnp.dot(a, b, preferred_element_type=F32)


def _dot_nt(a, b):
    return lax.dot_general(a, b, (((1,), (1,)), ((), ())), preferred_element_type=F32)


def _pick(n, prefs):
    for p in prefs:
        if n % p == 0:
            return p
    return n


def _proj_kernel(*refs, sub, act, conv, blocks_per_seq):
    if conv:
        h_ref, wt_ref, cw_ref, cb_ref, o_ref, w_ref, pad_ref = refs
    else:
        h_ref, wt_ref, o_ref, w_ref, acc0_ref, acc1_ref = refs
        acc_refs = (acc0_ref, acc1_ref)
    tm = h_ref.shape[0]
    nsub = tm // sub
    i = pl.program_id(1)

    @pl.when(i == 0)
    def _():
        w_ref[...] = wt_ref[...].T.astype(BF16)

    if conv:
        @pl.when(i % blocks_per_seq == 0)
        def _():
            pad_ref[:, 0:2 * SUBLANES, :] = jnp.zeros((pad_ref.shape[0], 2 * SUBLANES, LANES), F32)

        for r in range(nsub):
            rows = slice(r * sub, (r + 1) * sub)
            acc = _dot(h_ref[rows, :], w_ref[...])
            for c in range(pad_ref.shape[0]):
                cols = slice(c * LANES, (c + 1) * LANES)
                x = acc[:, cols]
                pad_ref[c, pl.ds(2 * SUBLANES, sub, stride=2), :] = x
                y = cb_ref[:, cols] + cw_ref[CONV_WIDTH - 1:CONV_WIDTH, cols] * x
                for s in range(1, CONV_WIDTH):
                    y = y + (cw_ref[CONV_WIDTH - 1 - s:CONV_WIDTH - s, cols]
                             * pad_ref[c, pl.ds(2 * (SUBLANES - s), sub, stride=2), :])
                pad_ref[c, pl.ds(0, SUBLANES, stride=2), :] = x[sub - SUBLANES:, :]
                o_ref[rows, cols] = (_silu(y) if act == "silu" else y).astype(o_ref.dtype)
        return

    for r in range(nsub + 1):
        if r > 0:
            acc = acc_refs[(r - 1) % 2][...]
            o_ref[(r - 1) * sub:r * sub, :] = (_silu(acc) if act == "silu" else _sigmoid(acc)).astype(o_ref.dtype)
        if r < nsub:
            acc_refs[r % 2][...] = _dot(h_ref[r * sub:(r + 1) * sub, :], w_ref[...])


def _proj(h, wt, row0, ncols, tn, seq, name, act=None, conv_w=None, conv_b=None, out_dtype=BF16):
    t, k = h.shape
    tm = _pick(seq, (2048, 1024, 512, 256, 128))
    conv = conv_w is not None
    sub = min(tm, 512 if conv else 256)
    in_specs = [pl.BlockSpec((tm, k), lambda j, i: (i, 0)),
                pl.BlockSpec((pl.Element(tn), pl.Element(k)),
                             lambda j, i: (pl.multiple_of(row0 + j * tn, SUBLANES), 0))]
    args = [h, wt]
    scratch = [pltpu.VMEM((k, tn), BF16)]
    if conv:
        in_specs += [pl.BlockSpec((CONV_WIDTH, tn), lambda j, i: (0, j)),
                     pl.BlockSpec((1, tn), lambda j, i: (0, j))]
        args += [conv_w, conv_b.reshape(1, ncols)]
        scratch += [pltpu.VMEM((tn // LANES, 2 * (sub + SUBLANES), LANES), F32)]
    else:
        scratch += [pltpu.VMEM((sub, tn), F32), pltpu.VMEM((sub, tn), F32)]
    return pl.pallas_call(
        functools.partial(_proj_kernel, sub=sub, act=act, conv=conv, blocks_per_seq=seq // tm),
        grid=(ncols // tn, t // tm),
        in_specs=in_specs,
        out_specs=pl.BlockSpec((tm, tn), lambda j, i: (i, j)),
        out_shape=jax.ShapeDtypeStruct((t, ncols), out_dtype),
        scratch_shapes=scratch,
        compiler_params=_cparams(("arbitrary", "arbitrary")),
        name=name,
    )(*args)


def _norm_q_dt_kernel(x_ref, g_ref, wqt_ref, wdtt_ref, h_ref, q_ref, dt_ref, wq_ref, wdt_ref, *, sub):
    tm = x_ref.shape[0]

    @pl.when(pl.program_id(0) == 0)
    def _():
        wq_ref[...] = wqt_ref[...].T.astype(BF16)
        wdt_ref[...] = wdtt_ref[...].T.astype(BF16)

    for r in range(tm // sub):
        rows = slice(r * sub, (r + 1) * sub)
        x = x_ref[rows, :]
        ms = jnp.mean(x * x, axis=-1, keepdims=True)
        h = (x * lax.rsqrt(ms + EPS) * g_ref[...]).astype(BF16)
        h_ref[rows, :] = h
        q_ref[rows, :] = _dot(h, wq_ref[...]).astype(q_ref.dtype)
        dt_ref[rows, :] = _dot(h, wdt_ref[...])


def _norm_q_dt(x2d, g, wt, row0, ncols, dt_row0, seq):
    t, k = x2d.shape
    tm = _pick(seq, (1024, 512, 256, 128))
    sub = min(tm, 256)
    return pl.pallas_call(
        functools.partial(_norm_q_dt_kernel, sub=sub),
        grid=(t // tm,),
        in_specs=[pl.BlockSpec((tm, k), lambda i: (i, 0)),
                  pl.BlockSpec((1, k), lambda i: (0, 0)),
                  pl.BlockSpec((pl.Element(ncols), pl.Element(k)), lambda i: (row0, 0)),
                  pl.BlockSpec((pl.Element(LANES), pl.Element(k)), lambda i: (dt_row0, 0))],
        out_specs=[pl.BlockSpec((tm, k), lambda i: (i, 0)),
                   pl.BlockSpec((tm, ncols), lambda i: (i, 0)),
                   pl.BlockSpec((tm, LANES), lambda i: (i, 0))],
        out_shape=[jax.ShapeDtypeStruct((t, k), BF16),
                   jax.ShapeDtypeStruct((t, ncols), BF16),
                   jax.ShapeDtypeStruct((t, LANES), F32)],
        scratch_shapes=[pltpu.VMEM((k, ncols), BF16), pltpu.VMEM((k, LANES), BF16)],
        compiler_params=_cparams(("arbitrary",)),
        name="norm_q_dt",
    )(x2d, g.reshape(1, k), wt, wt)


def _ssd_kernel(zs_ref, xbc_ref, dt_ref, dtb_ref, alog_ref, dexp_ref, ng_ref, e_ref,
                o_ref, state_ref, yacc_ref, rowterm_ref, wgt_ref, acs_ref, eahi_ref, ealo_ref,
                *, width, groups, nstate, heads):
    c = pl.program_id(1)
    L = SSD_CHUNK
    gw = width // groups
    pairs = gw // LANES
    b_off = width
    c_off = width + groups * nstate

    @pl.when(c == 0)
    def _():
        state_ref[...] = jnp.zeros(state_ref.shape, F32)

    row = lax.broadcasted_iota(jnp.int32, (L, L), 0)
    col = lax.broadcasted_iota(jnp.int32, (L, L), 1)
    causal = row >= col
    triu = (row <= col).astype(BF16)
    lo_half = lax.broadcasted_iota(jnp.int32, (L, LANES), 1) < SSD_HEAD_DIM

    nchunks = zs_ref.shape[0] // L
    a_neg = -jnp.exp(alog_ref[...])
    dt_ts = [_softplus(dt_ref[k * L:(k + 1) * L, :].T[:heads, :] + dtb_ref[...]) for k in range(nchunks)]
    parts = [_split3(dt_t * a_neg) for dt_t in dt_ts]
    acs_ts = [sum(_dot(part, triu) for part in p) for p in parts]
    for k in range(nchunks):
        acs_t, dt_t = acs_ts[k], dt_ts[k]
        rowterm_ref[k] = (acs_t - jnp.log(dt_t)) * LOG2E
        wgt_ref[k] = dt_t * jnp.exp(acs_t[:, L - 1:L] - acs_t)
        acs_n = jnp.concatenate([acs_t * LOG2E, jnp.zeros((LANES - heads, L), F32)], axis=0).T
        acs_ref[k] = acs_n
        ea_hi, ea_lo = _split2(jnp.exp2(acs_n))
        eahi_ref[k] = ea_hi
        ealo_ref[k] = ea_lo

    def chunk(ci, carry):
        rows = pl.ds(pl.multiple_of(ci * L, L), L)
        rowterm_t = rowterm_ref[ci]
        wgt_t = wgt_ref[ci]
        acs_n = acs_ref[ci]
        ea_hi = eahi_ref[ci]
        ea_lo = ealo_ref[ci]

        for g in range(groups):
            gcols = slice(g * gw, (g + 1) * gw)
            cg = xbc_ref[rows, c_off + g * nstate:c_off + (g + 1) * nstate]
            bg = xbc_ref[rows, b_off + g * nstate:b_off + (g + 1) * nstate]
            cb = _dot_nt(cg, bg)
            bgt = bg.astype(F32).T
            ea_x = _dot(ea_hi, e_ref[:, gcols]) + _dot(ea_lo, e_ref[:, gcols])
            yacc_ref[g] = _dot(cg, state_ref[g].astype(BF16)) * ea_x
            ssq = jnp.zeros((L, LANES), F32)
            for j in range(pairs):
                h0 = (g * pairs + j) * 2
                tile = slice(j * LANES, (j + 1) * LANES)
                gtile = slice(g * gw + j * LANES, g * gw + (j + 1) * LANES)
                top, bot = [], []
                for h in (h0, h0 + 1):
                    seg = acs_n[:, h:h + 1] - rowterm_t[h:h + 1, :]
                    top.append((jnp.exp2(jnp.where(causal, seg, -jnp.inf)) * cb).astype(BF16))
                    bot.append((bgt * wgt_t[h:h + 1, :]).astype(BF16))
                lhs = jnp.concatenate([jnp.concatenate(top, axis=1), jnp.concatenate(bot, axis=1)],
                                      axis=0)
                xp = xbc_ref[rows, gtile]
                zero = jnp.zeros_like(xp)
                rhs = jnp.concatenate([jnp.where(lo_half, xp, zero), jnp.where(lo_half, zero, xp)], axis=0)
                res = _dot(lhs, rhs)
                state_ref[g, :, tile] = state_ref[g, :, tile] * ea_x[L - 1:L, tile] + res[L:, :]
                y = res[:L, :] + yacc_ref[g, :, tile] + xp.astype(F32) * dexp_ref[:, gtile]
                y = y * zs_ref[rows, gtile].astype(F32)
                yacc_ref[g, :, tile] = y
                ssq = ssq + y * y
            scale = lax.rsqrt(jnp.sum(ssq, axis=-1, keepdims=True) * (1.0 / gw) + EPS)
            o_ref[rows, gcols] = (yacc_ref[g] * scale * ng_ref[:, gcols]).astype(o_ref.dtype)
        return carry

    lax.fori_loop(0, zs_ref.shape[0] // L, chunk, 0, unroll=4)


def _ssd_branch(zs, xbc, dt_raw, dt_bias, a_log, d_skip, norm_g, bsz, seq):
    t, width = zs.shape
    conv_ch = xbc.shape[1]
    heads = dt_bias.shape[0]
    groups = SSD_GROUPS
    nstate = SSD_STATE
    L = SSD_CHUNK
    rows = _pick(seq, (8 * L, 4 * L, 2 * L, L))
    nb = seq // rows
    dtb = jnp.broadcast_to(dt_bias[:, None], (heads, L))
    alog = jnp.broadcast_to(a_log[:, None], (heads, L))
    dexp = jnp.repeat(d_skip, SSD_HEAD_DIM).reshape(1, width)
    expand = (jnp.arange(LANES)[:, None] == (jnp.arange(width) // SSD_HEAD_DIM)[None, :]).astype(BF16)
    row = lambda b, c: (b * nb + c, 0)
    const = lambda b, c: (0, 0)
    return pl.pallas_call(
        functools.partial(_ssd_kernel, width=width, groups=groups, nstate=nstate, heads=heads),
        grid=(bsz, nb),
        in_specs=[pl.BlockSpec((rows, width), row),
                  pl.BlockSpec((rows, conv_ch), row),
                  pl.BlockSpec((rows, LANES), row),
                  pl.BlockSpec((heads, L), const),
                  pl.BlockSpec((heads, L), const),
                  pl.BlockSpec((1, width), const),
                  pl.BlockSpec((1, width), const),
                  pl.BlockSpec((LANES, width), const)],
        out_specs=pl.BlockSpec((rows, width), row),
        out_shape=jax.ShapeDtypeStruct((t, width), BF16),
        scratch_shapes=[pltpu.VMEM((groups, nstate, width // groups), F32),
                        pltpu.VMEM((groups, L, width // groups), F32),
                        pltpu.VMEM((rows // L, heads, L), F32),
                        pltpu.VMEM((rows // L, heads, L), F32),
                        pltpu.VMEM((rows // L, L, LANES), F32),
                        pltpu.VMEM((rows // L, L, LANES), BF16),
                        pltpu.VMEM((rows // L, L, LANES), BF16)],
        compiler_params=_cparams(("arbitrary", "arbitrary")),
        name="ssd",
    )(zs, xbc, dt_raw, dtb, alog, dexp, norm_g.reshape(1, width), expand)


def _lru_kernel(slg_ref, xl_ref, wg_ref, ba_ref, bx_ref, lam_ref, o_ref, a_ref, u_ref, carry_ref, *, rt):
    c = pl.program_id(1)
    rows, width = xl_ref.shape
    nt = width // LANES
    tg = width // LRU_TILE_GROUPS
    tiles_per_group = tg // LANES

    @pl.when(c == 0)
    def _():
        carry_ref[...] = jnp.zeros(carry_ref.shape, F32)

    def gate_tile(ti, carry):
        rs = pl.ds(pl.multiple_of(ti * rt, rt), rt)
        for g in range(LRU_TILE_GROUPS):
            gs = slice(g * tg, (g + 1) * tg)
            xl = xl_ref[rs, gs]
            gg = _dot(xl, wg_ref[g])
            tr = jnp.tanh(gg[:, :tg] + ba_ref[:, gs])
            ti = jnp.tanh(gg[:, tg:] + bx_ref[:, gs])
            ch = (-0.5 * LRU_C * LOG2E) * _softplus(-lam_ref[:, gs])
            a = jnp.exp2(tr * ch + ch)
            u = _sqrt_nonneg(1.0 - a * a) * ((0.5 * ti + 0.5) * xl.astype(F32))
            for k in range(tiles_per_group):
                a_ref[g * tiles_per_group + k, rs, :] = a[:, k * LANES:(k + 1) * LANES]
                u_ref[g * tiles_per_group + k, rs, :] = u[:, k * LANES:(k + 1) * LANES]
        return carry

    lax.fori_loop(0, rows // rt, gate_tile, 0, unroll=4)

    sub = lax.broadcasted_iota(jnp.int32, (SUBLANES, LANES), 0)

    def mini(m, carry):
        base = pl.multiple_of(m * SCAN_ROWS, SCAN_ROWS)
        for t in range(nt):
            cprev = carry_ref[:, t * LANES:(t + 1) * LANES]
            av = [a_ref[t, pl.ds(base + j, SUBLANES, stride=SCAN_STRIDE), :] for j in range(SCAN_STRIDE)]
            uv = [u_ref[t, pl.ds(base + j, SUBLANES, stride=SCAN_STRIDE), :] for j in range(SCAN_STRIDE)]
            h, p = uv[0], av[0]
            for j in range(1, SCAN_STRIDE):
                h = av[j] * h + uv[j]
                p = p * av[j]
            d = 1
            while d < SUBLANES:
                hs = pltpu.roll(h, d, 0)
                ps = pltpu.roll(p, d, 0)
                keep = sub < d
                h = jnp.where(keep, h, p * hs + h)
                p = jnp.where(keep, p, p * ps)
                d *= 2
            e = p * cprev + h
            hh = jnp.where(sub == 0, cprev, pltpu.roll(e, 1, 0))
            for j in range(SCAN_STRIDE):
                hh = av[j] * hh + uv[j]
                u_ref[t, pl.ds(base + j, SUBLANES, stride=SCAN_STRIDE), :] = hh
            carry_ref[:, t * LANES:(t + 1) * LANES] = jnp.broadcast_to(e[SUBLANES - 1:SUBLANES, :],
                                                                       (SUBLANES, LANES))
        return carry

    lax.fori_loop(0, rows // SCAN_ROWS, mini, 0, unroll=2)

    for t in range(nt):
        ts = slice(t * LANES, (t + 1) * LANES)
        o_ref[:, ts] = (u_ref[t] * slg_ref[:, ts].astype(F32)).astype(o_ref.dtype)


def _lru_branch(slg, xl, w_a, b_a, w_x, b_x, lam, bsz, seq):
    t, width = xl.shape
    rows = _pick(seq, (1024, 512, 256, 128))
    nb = seq // rows
    nblk, blk = w_a.shape[0], w_a.shape[1]
    per = nblk // LRU_TILE_GROUPS
    tg = width // LRU_TILE_GROUPS

    def tile_diag(w):
        wt = w.reshape(LRU_TILE_GROUPS, per, blk, blk)
        eye = jnp.eye(per, dtype=w.dtype)
        return jnp.einsum("gpij,pq->gpiqj", wt, eye).reshape(LRU_TILE_GROUPS, tg, tg)

    wg = (0.5 * jnp.concatenate([tile_diag(w_a), tile_diag(w_x)], axis=2)).astype(BF16)
    b_a, b_x = 0.5 * b_a, 0.5 * b_x
    row = lambda b, c: (b * nb + c, 0)
    const = lambda b, c: (0, 0)
    return pl.pallas_call(
        functools.partial(_lru_kernel, rt=min(rows, 128)),
        grid=(bsz, nb),
        in_specs=[pl.BlockSpec((rows, width), row),
                  pl.BlockSpec((rows, width), row),
                  pl.BlockSpec((LRU_TILE_GROUPS, tg, 2 * tg), lambda b, c: (0, 0, 0)),
                  pl.BlockSpec((1, width), const),
                  pl.BlockSpec((1, width), const),
                  pl.BlockSpec((1, width), const)],
        out_specs=pl.BlockSpec((rows, width), row),
        out_shape=jax.ShapeDtypeStruct((t, width), BF16),
        scratch_shapes=[pltpu.VMEM((width // LANES, rows, LANES), F32),
                        pltpu.VMEM((width // LANES, rows, LANES), F32),
                        pltpu.VMEM((SUBLANES, width), F32)],
        compiler_params=_cparams(("arbitrary", "arbitrary")),
        name="lru",
    )(slg, xl, wg, b_a.reshape(1, width), b_x.reshape(1, width), lam.reshape(1, width))


def _kv_kernel(m_ref, g_ref, w_ref, o_ref):
    m = m_ref[...]
    ms = jnp.mean(m * m, axis=-1, keepdims=True)
    mn = (m * lax.rsqrt(ms + EPS) * g_ref[...]).astype(BF16)
    o_ref[...] = _dot(mn, w_ref[...]).astype(o_ref.dtype)


def _mem_kv(mem2d, g, w_kv_b, tn):
    m, d = mem2d.shape
    n = w_kv_b.shape[1]
    return pl.pallas_call(
        _kv_kernel,
        grid=(n // tn,),
        in_specs=[pl.BlockSpec((m, d), lambda j: (0, 0)),
                  pl.BlockSpec((1, d), lambda j: (0, 0)),
                  pl.BlockSpec((d, tn), lambda j: (0, j))],
        out_specs=pl.BlockSpec((m, tn), lambda j: (0, j)),
        out_shape=jax.ShapeDtypeStruct((m, n), BF16),
        compiler_params=_cparams(("parallel",)),
        name="mem_kv",
    )(mem2d, g.reshape(1, d), w_kv_b)


def _merge_kernel(ys_ref, yl_ref, q_ref, kv_ref, gt_ref, x_ref, ws_ref, wl_ref, wm_ref, wo_ref, fg_ref, o_ref,
                  ym_ref, *, final_norm):
    d = x_ref.shape[1]
    hd = d // MEM_HEADS
    scale = hd ** -0.5
    sub = min(q_ref.shape[0], 256)
    for bi in range(q_ref.shape[0] // sub):
        rows = slice(bi * sub, (bi + 1) * sub)
        for h in range(MEM_HEADS):
            hs = slice(h * hd, (h + 1) * hd)
            s = _dot_nt(q_ref[rows, hs], kv_ref[:, hs]) * scale
            p = jnp.exp(s - jnp.max(s, axis=-1, keepdims=True))
            l = jnp.sum(p, axis=-1, keepdims=True)
            o = _dot(p.astype(BF16), kv_ref[:, d + h * hd:d + (h + 1) * hd]) / l
            ym_ref[rows, hs] = o.astype(ym_ref.dtype)
    merged = (gt_ref[:, :d].astype(F32) * _dot(ys_ref[...], ws_ref[...])
              + gt_ref[:, d:2 * d].astype(F32) * _dot(yl_ref[...], wl_ref[...])
              + gt_ref[:, 2 * d:].astype(F32) * _dot(ym_ref[...], wm_ref[...]))
    xn = x_ref[...] + _dot(merged.astype(BF16), wo_ref[...])
    if final_norm:
        ms = jnp.mean(xn * xn, axis=-1, keepdims=True)
        xn = xn * lax.rsqrt(ms + EPS) * fg_ref[...]
    o_ref[...] = xn


def _merge(y_ssd, y_lru, q, kv, gates, x2d, w_s, w_l, w_m, w_o, final_g, final_norm, seq):
    t, d = x2d.shape
    tm = _pick(seq, (512, 256, 128))
    blocks_per_seq = seq // tm
    m = kv.shape[0] // (t // seq)
    row = lambda i: (i, 0)
    const = lambda i: (0, 0)
    full = lambda a: pl.BlockSpec(a.shape, const)
    return pl.pallas_call(
        functools.partial(_merge_kernel, final_norm=final_norm),
        grid=(t // tm,),
        in_specs=[pl.BlockSpec((tm, y_ssd.shape[1]), row),
                  pl.BlockSpec((tm, y_lru.shape[1]), row),
                  pl.BlockSpec((tm, d), row),
                  pl.BlockSpec((m, 2 * d), lambda i: (i // blocks_per_seq, 0)),
                  pl.BlockSpec((tm, gates.shape[1]), row),
                  pl.BlockSpec((tm, d), row),
                  full(w_s), full(w_l), full(w_m), full(w_o),
                  pl.BlockSpec((1, d), const)],
        out_specs=pl.BlockSpec((tm, d), row),
        out_shape=jax.ShapeDtypeStruct((t, d), F32),
        scratch_shapes=[pltpu.VMEM((tm, d), BF16)],
        compiler_params=_cparams(("parallel",)),
        name="merge",
    )(y_ssd, y_lru, q, kv, gates, x2d, w_s, w_l, w_m, w_o, final_g.reshape(1, d))


def kernel(x, mem, norm_g, w_in, ssd_conv_w, ssd_conv_b, ssd_dt_bias, ssd_a_log, ssd_d, ssd_norm_g, lru_conv_w, lru_conv_b, lru_w_a, lru_b_a, lru_w_x, lru_b_x, lru_lambda, mem_norm_g, w_kv, w_br_ssd, w_br_lru, w_br_mem, w_out, final_g):
    bsz, seq, d = x.shape
    depth = norm_g.shape[0]
    t = bsz * seq
    heads = ssd_dt_bias.shape[1]
    ssd_w = heads * SSD_HEAD_DIM
    conv_ch = ssd_conv_w.shape[2]
    lru_w = lru_conv_w.shape[2]
    x2d = x.reshape(t, d)
    mem2d = mem.reshape(bsz * mem.shape[1], d)
    front = ssd_w + conv_ch
    tn = lambda n: _pick(n, (1024, 768, 512, 256, 128))
    for l in range(depth):
        wt = jnp.swapaxes(w_in[l], 0, 1)
        rest = front + heads

        h, q, dt_raw = _norm_q_dt(x2d, norm_g[l], wt, rest + 2 * lru_w, d, front, seq)
        zs = _proj(h, wt, 0, ssd_w, tn(ssd_w), seq, "proj_z", act="silu")
        xbc = _proj(h, wt, ssd_w, conv_ch, tn(conv_ch), seq, "proj_xbc", act="silu",
                    conv_w=ssd_conv_w[l], conv_b=ssd_conv_b[l])
        slg = _proj(h, wt, rest, lru_w, tn(lru_w), seq, "proj_lg", act="silu")
        xl = _proj(h, wt, rest + lru_w, lru_w, tn(lru_w), seq, "proj_lx",
                   conv_w=lru_conv_w[l], conv_b=lru_conv_b[l])
        gates = _proj(h, wt, rest + 2 * lru_w + d, 3 * d, tn(3 * d), seq, "proj_gates", act="sigmoid")

        y_ssd = _ssd_branch(zs, xbc, dt_raw, ssd_dt_bias[l], ssd_a_log[l], ssd_d[l],
                            ssd_norm_g[l].reshape(-1), bsz, seq)
        y_lru = _lru_branch(slg, xl, lru_w_a[l], lru_b_a[l].reshape(-1), lru_w_x[l], lru_b_x[l].reshape(-1),
                            lru_lambda[l], bsz, seq)
        kv = _mem_kv(mem2d, mem_norm_g[l], w_kv[l].astype(BF16), _pick(2 * d, (512, 256, 128)))
        x2d = _merge(y_ssd, y_lru, q, kv, gates, x2d, w_br_ssd[l].astype(BF16), w_br_lru[l].astype(BF16),
                     w_br_mem[l].astype(BF16), w_out[l].astype(BF16), final_g, l == depth - 1, seq)
    return x2d.reshape(bsz, seq, d)
```

```python
import functools

import jax
import jax.numpy as jnp
from jax import lax
from jax.experimental import pallas as pl
from jax.experimental.pallas import tpu as pltpu

F32 = jnp.float32
BF16 = jnp.bfloat16

EPS = 1e-6
CONV_WIDTH = 4
SSD_HEAD_DIM = 64
SSD_GROUPS = 4
SSD_STATE = 128
SSD_CHUNK = 128
LRU_C = 8.0
MEM_HEADS = 4
LANES = 128
SUBLANES = 8
LRU_TILE_GROUPS = 4
SCAN_STRIDE = 4
SCAN_ROWS = SCAN_STRIDE * SUBLANES

VMEM_LIMIT = 56 * 1024 * 1024


def _cparams(sem):
    return pltpu.CompilerParams(dimension_semantics=sem, vmem_limit_bytes=VMEM_LIMIT)


LOG2E = 1.4426950408889634
TINY = 1e-30


def _sigmoid(x):
    return 1.0 / (1.0 + jnp.exp2(x * (-LOG2E)))


def _sqrt_nonneg(x):
    return x * lax.rsqrt(jnp.maximum(x, TINY))


def _silu(x):
    return x * _sigmoid(x)


def _softplus(x):
    return jnp.maximum(x, 0.0) + jnp.log1p(jnp.exp(-jnp.abs(x)))


def _split2(x):
    hi = x.astype(BF16)
    lo = (x - hi.astype(F32)).astype(BF16)
    return hi, lo


def _split3(x):
    hi = x.astype(BF16)
    r = x - hi.astype(F32)
    mid = r.astype(BF16)
    lo = (r - mid.astype(F32)).astype(BF16)
    return hi, mid, lo


def _dot(a, b):
    return jnp.dot(a, b, preferred_element_type=F32)


def _dot_nt(a, b):
    return lax.dot_general(a, b, (((1,), (1,)), ((), ())), preferred_element_type=F32)


def _pick(n, prefs):
    for p in prefs:
        if n % p == 0:
            return p
    return n


def _proj_kernel(*refs, sub, act, conv, blocks_per_seq):
    if conv:
        h_ref, wt_ref, cw_ref, cb_ref, o_ref, w_ref, pad_ref = refs
    else:
        h_ref, wt_ref, o_ref, w_ref, acc0_ref, acc1_ref = refs
        acc_refs = (acc0_ref, acc1_ref)
    tm = h_ref.shape[0]
    nsub = tm // sub
    i = pl.program_id(1)

    @pl.when(i == 0)
    def _():
        w_ref[...] = wt_ref[...].T.astype(BF16)

    if conv:
        @pl.when(i % blocks_per_seq == 0)
        def _():
            pad_ref[:, 0:2 * SUBLANES, :] = jnp.zeros((pad_ref.shape[0], 2 * SUBLANES, LANES), F32)

        for r in range(nsub):
            rows = slice(r * sub, (r + 1) * sub)
            acc = _dot(h_ref[rows, :], w_ref[...])
            for c in range(pad_ref.shape[0]):
                cols = slice(c * LANES, (c + 1) * LANES)
                x = acc[:, cols]
                pad_ref[c, pl.ds(2 * SUBLANES, sub, stride=2), :] = x
                y = cb_ref[:, cols] + cw_ref[CONV_WIDTH - 1:CONV_WIDTH, cols] * x
                for s in range(1, CONV_WIDTH):
                    y = y + (cw_ref[CONV_WIDTH - 1 - s:CONV_WIDTH - s, cols]
                             * pad_ref[c, pl.ds(2 * (SUBLANES - s), sub, stride=2), :])
                pad_ref[c, pl.ds(0, SUBLANES, stride=2), :] = x[sub - SUBLANES:, :]
                o_ref[rows, cols] = (_silu(y) if act == "silu" else y).astype(o_ref.dtype)
        return

    for r in range(nsub + 1):
        if r > 0:
            acc = acc_refs[(r - 1) % 2][...]
            o_ref[(r - 1) * sub:r * sub, :] = (_silu(acc) if act == "silu" else _sigmoid(acc)).astype(o_ref.dtype)
        if r < nsub:
            acc_refs[r % 2][...] = _dot(h_ref[r * sub:(r + 1) * sub, :], w_ref[...])


def _proj(h, wt, row0, ncols, tn, seq, name, act=None, conv_w=None, conv_b=None, out_dtype=BF16):
    t, k = h.shape
    tm = _pick(seq, (2048, 1024, 512, 256, 128))
    conv = conv_w is not None
    sub = min(tm, 128)
    in_specs = [pl.BlockSpec((tm, k), lambda j, i: (i, 0)),
                pl.BlockSpec((pl.Element(tn), pl.Element(k)),
                             lambda j, i: (pl.multiple_of(row0 + j * tn, SUBLANES), 0))]
    args = [h, wt]
    scratch = [pltpu.VMEM((k, tn), BF16)]
    if conv:
        in_specs += [pl.BlockSpec((CONV_WIDTH, tn), lambda j, i: (0, j)),
                     pl.BlockSpec((1, tn), lambda j, i: (0, j))]
        args += [conv_w, conv_b.reshape(1, ncols)]
        scratch += [pltpu.VMEM((tn // LANES, 2 * (sub + SUBLANES), LANES), F32)]
    else:
        scratch += [pltpu.VMEM((sub, tn), F32), pltpu.VMEM((sub, tn), F32)]
    return pl.pallas_call(
        functools.partial(_proj_kernel, sub=sub, act=act, conv=conv, blocks_per_seq=seq // tm),
        grid=(ncols // tn, t // tm),
        in_specs=in_specs,
        out_specs=pl.BlockSpec((tm, tn), lambda j, i: (i, j)),
        out_shape=jax.ShapeDtypeStruct((t, ncols), out_dtype),
        scratch_shapes=scratch,
        compiler_params=_cparams(("arbitrary", "arbitrary")),
        name=name,
    )(*args)


def _norm_q_dt_kernel(x_ref, g_ref, wqt_ref, wdtt_ref, h_ref, q_ref, dt_ref, wq_ref, wdt_ref, *, sub):
    tm = x_ref.shape[0]

    @pl.when(pl.program_id(0) == 0)
    def _():
        wq_ref[...] = wqt_ref[...].T.astype(BF16)
        wdt_ref[...] = wdtt_ref[...].T.astype(BF16)

    for r in range(tm // sub):
        rows = slice(r * sub, (r + 1) * sub)
        x = x_ref[rows, :]
        ms = jnp.mean(x * x, axis=-1, keepdims=True)
        h = (x * lax.rsqrt(ms + EPS) * g_ref[...]).astype(BF16)
        h_ref[rows, :] = h
        q_ref[rows, :] = _dot(h, wq_ref[...]).astype(q_ref.dtype)
        dt_ref[rows, :] = _dot(h, wdt_ref[...])


def _norm_q_dt(x2d, g, wt, row0, ncols, dt_row0, seq):
    t, k = x2d.shape
    tm = _pick(seq, (1024, 512, 256, 128))
    sub = min(tm, 256)
    return pl.pallas_call(
        functools.partial(_norm_q_dt_kernel, sub=sub),
        grid=(t // tm,),
        in_specs=[pl.BlockSpec((tm, k), lambda i: (i, 0)),
                  pl.BlockSpec((1, k), lambda i: (0, 0)),
                  pl.BlockSpec((pl.Element(ncols), pl.Element(k)), lambda i: (row0, 0)),
                  pl.BlockSpec((pl.Element(LANES), pl.Element(k)), lambda i: (dt_row0, 0))],
        out_specs=[pl.BlockSpec((tm, k), lambda i: (i, 0)),
                   pl.BlockSpec((tm, ncols), lambda i: (i, 0)),
                   pl.BlockSpec((tm, LANES), lambda i: (i, 0))],
        out_shape=[jax.ShapeDtypeStruct((t, k), BF16),
                   jax.ShapeDtypeStruct((t, ncols), BF16),
                   jax.ShapeDtypeStruct((t, LANES), F32)],
        scratch_shapes=[pltpu.VMEM((k, ncols), BF16), pltpu.VMEM((k, LANES), BF16)],
        compiler_params=_cparams(("arbitrary",)),
        name="norm_q_dt",
    )(x2d, g.reshape(1, k), wt, wt)


def _ssd_kernel(zs_ref, xbc_ref, dt_ref, dtb_ref, alog_ref, dexp_ref, ng_ref, e_ref,
                o_ref, state_ref, yacc_ref, rowterm_ref, wgt_ref, acs_ref, eahi_ref, ealo_ref,
                *, width, groups, nstate, heads):
    c = pl.program_id(1)
    L = SSD_CHUNK
    gw = width // groups
    pairs = gw // LANES
    b_off = width
    c_off = width + groups * nstate

    @pl.when(c == 0)
    def _():
        state_ref[...] = jnp.zeros(state_ref.shape, F32)

    row = lax.broadcasted_iota(jnp.int32, (L, L), 0)
    col = lax.broadcasted_iota(jnp.int32, (L, L), 1)
    causal = row >= col
    triu = (row <= col).astype(BF16)
    lo_half = lax.broadcasted_iota(jnp.int32, (L, LANES), 1) < SSD_HEAD_DIM

    nchunks = zs_ref.shape[0] // L
    a_neg = -jnp.exp(alog_ref[...])
    dt_ts = [_softplus(dt_ref[k * L:(k + 1) * L, :].T[:heads, :] + dtb_ref[...]) for k in range(nchunks)]
    parts = [_split3(dt_t * a_neg) for dt_t in dt_ts]
    acs_ts = [sum(_dot(part, triu) for part in p) for p in parts]
    for k in range(nchunks):
        acs_t, dt_t = acs_ts[k], dt_ts[k]
        rowterm_ref[k] = (acs_t - jnp.log(dt_t)) * LOG2E
        wgt_ref[k] = dt_t * jnp.exp(acs_t[:, L - 1:L] - acs_t)
        acs_n = jnp.concatenate([acs_t * LOG2E, jnp.zeros((LANES - heads, L), F32)], axis=0).T
        acs_ref[k] = acs_n
        ea_hi, ea_lo = _split2(jnp.exp2(acs_n))
        eahi_ref[k] = ea_hi
        ealo_ref[k] = ea_lo

    def chunk(ci, carry):
        rows = pl.ds(pl.multiple_of(ci * L, L), L)
        rowterm_t = rowterm_ref[ci]
        wgt_t = wgt_ref[ci]
        acs_n = acs_ref[ci]
        ea_hi = eahi_ref[ci]
        ea_lo = ealo_ref[ci]

        for g in range(groups):
            gcols = slice(g * gw, (g + 1) * gw)
            cg = xbc_ref[rows, c_off + g * nstate:c_off + (g + 1) * nstate]
            bg = xbc_ref[rows, b_off + g * nstate:b_off + (g + 1) * nstate]
            cb = _dot_nt(cg, bg)
            bgt = bg.astype(F32).T
            ea_x = _dot(ea_hi, e_ref[:, gcols]) + _dot(ea_lo, e_ref[:, gcols])
            yacc_ref[g] = _dot(cg, state_ref[g].astype(BF16)) * ea_x
            ssq = jnp.zeros((L, LANES), F32)
            for j in range(pairs):
                h0 = (g * pairs + j) * 2
                tile = slice(j * LANES, (j + 1) * LANES)
                gtile = slice(g * gw + j * LANES, g * gw + (j + 1) * LANES)
                top, bot = [], []
                for h in (h0, h0 + 1):
                    seg = acs_n[:, h:h + 1] - rowterm_t[h:h + 1, :]
                    top.append((jnp.exp2(jnp.where(causal, seg, -jnp.inf)) * cb).astype(BF16))
                    bot.append((bgt * wgt_t[h:h + 1, :]).astype(BF16))
                lhs = jnp.concatenate([jnp.concatenate(top, axis=1), jnp.concatenate(bot, axis=1)],
                                      axis=0)
                xp = xbc_ref[rows, gtile]
                zero = jnp.zeros_like(xp)
                rhs = jnp.concatenate([jnp.where(lo_half, xp, zero), jnp.where(lo_half, zero, xp)], axis=0)
                res = _dot(lhs, rhs)
                state_ref[g, :, tile] = state_ref[g, :, tile] * ea_x[L - 1:L, tile] + res[L:, :]
                y = res[:L, :] + yacc_ref[g, :, tile] + xp.astype(F32) * dexp_ref[:, gtile]
                y = y * zs_ref[rows, gtile].astype(F32)
                yacc_ref[g, :, tile] = y
                ssq = ssq + y * y
            scale = lax.rsqrt(jnp.sum(ssq, axis=-1, keepdims=True) * (1.0 / gw) + EPS)
            o_ref[rows, gcols] = (yacc_ref[g] * scale * ng_ref[:, gcols]).astype(o_ref.dtype)
        return carry

    lax.fori_loop(0, zs_ref.shape[0] // L, chunk, 0, unroll=4)


def _ssd_branch(zs, xbc, dt_raw, dt_bias, a_log, d_skip, norm_g, bsz, seq):
    t, width = zs.shape
    conv_ch = xbc.shape[1]
    heads = dt_bias.shape[0]
    groups = SSD_GROUPS
    nstate = SSD_STATE
    L = SSD_CHUNK
    rows = _pick(seq, (8 * L, 4 * L, 2 * L, L))
    nb = seq // rows
    dtb = jnp.broadcast_to(dt_bias[:, None], (heads, L))
    alog = jnp.broadcast_to(a_log[:, None], (heads, L))
    dexp = jnp.repeat(d_skip, SSD_HEAD_DIM).reshape(1, width)
    expand = (jnp.arange(LANES)[:, None] == (jnp.arange(width) // SSD_HEAD_DIM)[None, :]).astype(BF16)
    row = lambda b, c: (b * nb + c, 0)
    const = lambda b, c: (0, 0)
    return pl.pallas_call(
        functools.partial(_ssd_kernel, width=width, groups=groups, nstate=nstate, heads=heads),
        grid=(bsz, nb),
        in_specs=[pl.BlockSpec((rows, width), row),
                  pl.BlockSpec((rows, conv_ch), row),
                  pl.BlockSpec((rows, LANES), row),
                  pl.BlockSpec((heads, L), const),
                  pl.BlockSpec((heads, L), const),
                  pl.BlockSpec((1, width), const),
                  pl.BlockSpec((1, width), const),
                  pl.BlockSpec((LANES, width), const)],
        out_specs=pl.BlockSpec((rows, width), row),
        out_shape=jax.ShapeDtypeStruct((t, width), BF16),
        scratch_shapes=[pltpu.VMEM((groups, nstate, width // groups), F32),
                        pltpu.VMEM((groups, L, width // groups), F32),
                        pltpu.VMEM((rows // L, heads, L), F32),
                        pltpu.VMEM((rows // L, heads, L), F32),
                        pltpu.VMEM((rows // L, L, LANES), F32),
                        pltpu.VMEM((rows // L, L, LANES), BF16),
                        pltpu.VMEM((rows // L, L, LANES), BF16)],
        compiler_params=_cparams(("arbitrary", "arbitrary")),
        name="ssd",
    )(zs, xbc, dt_raw, dtb, alog, dexp, norm_g.reshape(1, width), expand)


def _lru_kernel(slg_ref, xl_ref, wg_ref, ba_ref, bx_ref, lam_ref, o_ref, a_ref, u_ref, carry_ref, *, rt):
    c = pl.program_id(1)
    rows, width = xl_ref.shape
    nt = width // LANES
    tg = width // LRU_TILE_GROUPS
    tiles_per_group = tg // LANES

    @pl.when(c == 0)
    def _():
        carry_ref[...] = jnp.zeros(carry_ref.shape, F32)

    def gate_tile(ti, carry):
        rs = pl.ds(pl.multiple_of(ti * rt, rt), rt)
        for g in range(LRU_TILE_GROUPS):
            gs = slice(g * tg, (g + 1) * tg)
            xl = xl_ref[rs, gs]
            gg = _dot(xl, wg_ref[g])
            tr = jnp.tanh(gg[:, :tg] + ba_ref[:, gs])
            ti = jnp.tanh(gg[:, tg:] + bx_ref[:, gs])
            ch = (-0.5 * LRU_C * LOG2E) * _softplus(-lam_ref[:, gs])
            a = jnp.exp2(tr * ch + ch)
            u = _sqrt_nonneg(1.0 - a * a) * ((0.5 * ti + 0.5) * xl.astype(F32))
            for k in range(tiles_per_group):
                a_ref[g * tiles_per_group + k, rs, :] = a[:, k * LANES:(k + 1) * LANES]
                u_ref[g * tiles_per_group + k, rs, :] = u[:, k * LANES:(k + 1) * LANES]
        return carry

    lax.fori_loop(0, rows // rt, gate_tile, 0, unroll=4)

    sub = lax.broadcasted_iota(jnp.int32, (SUBLANES, LANES), 0)

    def mini(m, carry):
        base = pl.multiple_of(m * SCAN_ROWS, SCAN_ROWS)
        for t in range(nt):
            cprev = carry_ref[:, t * LANES:(t + 1) * LANES]
            av = [a_ref[t, pl.ds(base + j, SUBLANES, stride=SCAN_STRIDE), :] for j in range(SCAN_STRIDE)]
            uv = [u_ref[t, pl.ds(base + j, SUBLANES, stride=SCAN_STRIDE), :] for j in range(SCAN_STRIDE)]
            h, p = uv[0], av[0]
            for j in range(1, SCAN_STRIDE):
                h = av[j] * h + uv[j]
                p = p * av[j]
            d = 1
            while d < SUBLANES:
                hs = pltpu.roll(h, d, 0)
                ps = pltpu.roll(p, d, 0)
                keep = sub < d
                h = jnp.where(keep, h, p * hs + h)
                p = jnp.where(keep, p, p * ps)
                d *= 2
            e = p * cprev + h
            hh = jnp.where(sub == 0, cprev, pltpu.roll(e, 1, 0))
            for j in range(SCAN_STRIDE):
                hh = av[j] * hh + uv[j]
                u_ref[t, pl.ds(base + j, SUBLANES, stride=SCAN_STRIDE), :] = hh
            carry_ref[:, t * LANES:(t + 1) * LANES] = jnp.broadcast_to(e[SUBLANES - 1:SUBLANES, :],
                                                                       (SUBLANES, LANES))
        return carry

    lax.fori_loop(0, rows // SCAN_ROWS, mini, 0, unroll=4)

    for t in range(nt):
        ts = slice(t * LANES, (t + 1) * LANES)
        o_ref[:, ts] = (u_ref[t] * slg_ref[:, ts].astype(F32)).astype(o_ref.dtype)


def _lru_branch(slg, xl, w_a, b_a, w_x, b_x, lam, bsz, seq):
    t, width = xl.shape
    rows = _pick(seq, (1024, 512, 256, 128))
    nb = seq // rows
    nblk, blk = w_a.shape[0], w_a.shape[1]
    per = nblk // LRU_TILE_GROUPS
    tg = width // LRU_TILE_GROUPS

    def tile_diag(w):
        wt = w.reshape(LRU_TILE_GROUPS, per, blk, blk)
        eye = jnp.eye(per, dtype=w.dtype)
        return jnp.einsum("gpij,pq->gpiqj", wt, eye).reshape(LRU_TILE_GROUPS, tg, tg)

    wg = (0.5 * jnp.concatenate([tile_diag(w_a), tile_diag(w_x)], axis=2)).astype(BF16)
    b_a, b_x = 0.5 * b_a, 0.5 * b_x
    row = lambda b, c: (b * nb + c, 0)
    const = lambda b, c: (0, 0)
    return pl.pallas_call(
        functools.partial(_lru_kernel, rt=min(rows, 128)),
        grid=(bsz, nb),
        in_specs=[pl.BlockSpec((rows, width), row),
                  pl.BlockSpec((rows, width), row),
                  pl.BlockSpec((LRU_TILE_GROUPS, tg, 2 * tg), lambda b, c: (0, 0, 0)),
                  pl.BlockSpec((1, width), const),
                  pl.BlockSpec((1, width), const),
                  pl.BlockSpec((1, width), const)],
        out_specs=pl.BlockSpec((rows, width), row),
        out_shape=jax.ShapeDtypeStruct((t, width), BF16),
        scratch_shapes=[pltpu.VMEM((width // LANES, rows, LANES), F32),
                        pltpu.VMEM((width // LANES, rows, LANES), F32),
                        pltpu.VMEM((SUBLANES, width), F32)],
        compiler_params=_cparams(("arbitrary", "arbitrary")),
        name="lru",
    )(slg, xl, wg, b_a.reshape(1, width), b_x.reshape(1, width), lam.reshape(1, width))


def _kv_kernel(m_ref, g_ref, w_ref, o_ref):
    m = m_ref[...]
    ms = jnp.mean(m * m, axis=-1, keepdims=True)
    mn = (m * lax.rsqrt(ms + EPS) * g_ref[...]).astype(BF16)
    o_ref[...] = _dot(mn, w_ref[...]).astype(o_ref.dtype)


def _mem_kv(mem2d, g, w_kv_b, tn):
    m, d = mem2d.shape
    n = w_kv_b.shape[1]
    return pl.pallas_call(
        _kv_kernel,
        grid=(n // tn,),
        in_specs=[pl.BlockSpec((m, d), lambda j: (0, 0)),
                  pl.BlockSpec((1, d), lambda j: (0, 0)),
                  pl.BlockSpec((d, tn), lambda j: (0, j))],
        out_specs=pl.BlockSpec((m, tn), lambda j: (0, j)),
        out_shape=jax.ShapeDtypeStruct((m, n), BF16),
        compiler_params=_cparams(("parallel",)),
        name="mem_kv",
    )(mem2d, g.reshape(1, d), w_kv_b)


def _merge_kernel(ys_ref, yl_ref, q_ref, kv_ref, gt_ref, x_ref, ws_ref, wl_ref, wm_ref, wo_ref, fg_ref, o_ref,
                  ym_ref, *, final_norm):
    d = x_ref.shape[1]
    hd = d // MEM_HEADS
    scale = hd ** -0.5
    sub = min(q_ref.shape[0], 256)
    for bi in range(q_ref.shape[0] // sub):
        rows = slice(bi * sub, (bi + 1) * sub)
        for h in range(MEM_HEADS):
            hs = slice(h * hd, (h + 1) * hd)
            s = _dot_nt(q_ref[rows, hs], kv_ref[:, hs]) * scale
            p = jnp.exp(s - jnp.max(s, axis=-1, keepdims=True))
            l = jnp.sum(p, axis=-1, keepdims=True)
            o = _dot(p.astype(BF16), kv_ref[:, d + h * hd:d + (h + 1) * hd]) / l
            ym_ref[rows, hs] = o.astype(ym_ref.dtype)
    merged = (gt_ref[:, :d].astype(F32) * _dot(ys_ref[...], ws_ref[...])
              + gt_ref[:, d:2 * d].astype(F32) * _dot(yl_ref[...], wl_ref[...])
              + gt_ref[:, 2 * d:].astype(F32) * _dot(ym_ref[...], wm_ref[...]))
    xn = x_ref[...] + _dot(merged.astype(BF16), wo_ref[...])
    if final_norm:
        ms = jnp.mean(xn * xn, axis=-1, keepdims=True)
        xn = xn * lax.rsqrt(ms + EPS) * fg_ref[...]
    o_ref[...] = xn


def _merge(y_ssd, y_lru, q, kv, gates, x2d, w_s, w_l, w_m, w_o, final_g, final_norm, seq):
    t, d = x2d.shape
    tm = _pick(seq, (512, 256, 128))
    blocks_per_seq = seq // tm
    m = kv.shape[0] // (t // seq)
    row = lambda i: (i, 0)
    const = lambda i: (0, 0)
    full = lambda a: pl.BlockSpec(a.shape, const)
    return pl.pallas_call(
        functools.partial(_merge_kernel, final_norm=final_norm),
        grid=(t // tm,),
        in_specs=[pl.BlockSpec((tm, y_ssd.shape[1]), row),
                  pl.BlockSpec((tm, y_lru.shape[1]), row),
                  pl.BlockSpec((tm, d), row),
                  pl.BlockSpec((m, 2 * d), lambda i: (i // blocks_per_seq, 0)),
                  pl.BlockSpec((tm, gates.shape[1]), row),
                  pl.BlockSpec((tm, d), row),
                  full(w_s), full(w_l), full(w_m), full(w_o),
                  pl.BlockSpec((1, d), const)],
        out_specs=pl.BlockSpec((tm, d), row),
        out_shape=jax.ShapeDtypeStruct((t, d), F32),
        scratch_shapes=[pltpu.VMEM((tm, d), BF16)],
        compiler_params=_cparams(("parallel",)),
        name="merge",
    )(y_ssd, y_lru, q, kv, gates, x2d, w_s, w_l, w_m, w_o, final_g.reshape(1, d))


def kernel(x, mem, norm_g, w_in, ssd_conv_w, ssd_conv_b, ssd_dt_bias, ssd_a_log, ssd_d, ssd_norm_g, lru_conv_w, lru_conv_b, lru_w_a, lru_b_a, lru_w_x, lru_b_x, lru_lambda, mem_norm_g, w_kv, w_br_ssd, w_br_lru, w_br_mem, w_out, final_g):
    bsz, seq, d = x.shape
    depth = norm_g.shape[0]
    t = bsz * seq
    heads = ssd_dt_bias.shape[1]
    ssd_w = heads * SSD_HEAD_DIM
    conv_ch = ssd_conv_w.shape[2]
    lru_w = lru_conv_w.shape[2]
    x2d = x.reshape(t, d)
    mem2d = mem.reshape(bsz * mem.shape[1], d)
    front = ssd_w + conv_ch
    tn = lambda n: _pick(n, (1024, 768, 512, 256, 128))
    for l in range(depth):
        wt = jnp.swapaxes(w_in[l], 0, 1)
        rest = front + heads

        h, q, dt_raw = _norm_q_dt(x2d, norm_g[l], wt, rest + 2 * lru_w, d, front, seq)
        zs = _proj(h, wt, 0, ssd_w, tn(ssd_w), seq, "proj_z", act="silu")
        xbc = _proj(h, wt, ssd_w, conv_ch, tn(conv_ch), seq, "proj_xbc", act="silu",
                    conv_w=ssd_conv_w[l], conv_b=ssd_conv_b[l])
        slg = _proj(h, wt, rest, lru_w, tn(lru_w), seq, "proj_lg", act="silu")
        xl = _proj(h, wt, rest + lru_w, lru_w, tn(lru_w), seq, "proj_lx",
                   conv_w=lru_conv_w[l], conv_b=lru_conv_b[l])
        gates = _proj(h, wt, rest + 2 * lru_w + d, 3 * d, tn(3 * d), seq, "proj_gates", act="sigmoid")

        y_ssd = _ssd_branch(zs, xbc, dt_raw, ssd_dt_bias[l], ssd_a_log[l], ssd_d[l],
                            ssd_norm_g[l].reshape(-1), bsz, seq)
        y_lru = _lru_branch(slg, xl, lru_w_a[l], lru_b_a[l].reshape(-1), lru_w_x[l], lru_b_x[l].reshape(-1),
                            lru_lambda[l], bsz, seq)
        kv = _mem_kv(mem2d, mem_norm_g[l], w_kv[l].astype(BF16), _pick(2 * d, (512, 256, 128)))
        x2d = _merge(y_ssd, y_lru, q, kv, gates, x2d, w_br_ssd[l].astype(BF16), w_br_lru[l].astype(BF16),
                     w_br_mem[l].astype(BF16), w_out[l].astype(BF16), final_g, l == depth - 1, seq)
    return x2d.reshape(bsz, seq, d)
```

```python
import functools

import jax
import jax.numpy as jnp
from jax import lax
from jax.experimental import pallas as pl
from jax.experimental.pallas import tpu as pltpu

F32 = jnp.float32
BF16 = jnp.bfloat16

EPS = 1e-6
CONV_WIDTH = 4
SSD_HEAD_DIM = 64
SSD_GROUPS = 4
SSD_STATE = 128
SSD_CHUNK = 128
LRU_C = 8.0
MEM_HEADS = 4
LANES = 128
SUBLANES = 8
LRU_TILE_GROUPS = 4
SCAN_STRIDE = 4
SCAN_ROWS = SCAN_STRIDE * SUBLANES

VMEM_LIMIT = 56 * 1024 * 1024


def _cparams(sem):
    return pltpu.CompilerParams(dimension_semantics=sem, vmem_limit_bytes=VMEM_LIMIT)


LOG2E = 1.4426950408889634
TINY = 1e-30


def _sigmoid(x):
    return 1.0 / (1.0 + jnp.exp2(x * (-LOG2E)))


def _sqrt_nonneg(x):
    return x * lax.rsqrt(jnp.maximum(x, TINY))


def _silu(x):
    return x * _sigmoid(x)


def _softplus(x):
    return jnp.maximum(x, 0.0) + jnp.log1p(jnp.exp(-jnp.abs(x)))


def _split2(x):
    hi = x.astype(BF16)
    lo = (x - hi.astype(F32)).astype(BF16)
    return hi, lo


def _split3(x):
    hi = x.astype(BF16)
    r = x - hi.astype(F32)
    mid = r.astype(BF16)
    lo = (r - mid.astype(F32)).astype(BF16)
    return hi, mid, lo


def _dot(a, b):
    return jnp.dot(a, b, preferred_element_type=F32)


def _dot_nt(a, b):
    return lax.dot_general(a, b, (((1,), (1,)), ((), ())), preferred_element_type=F32)


def _pick(n, prefs):
    for p in prefs:
        if n % p == 0:
            return p
    return n


def _proj_kernel(*refs, sub, act, conv, blocks_per_seq):
    if conv:
        h_ref, wt_ref, cw_ref, cb_ref, o_ref, w_ref, pad_ref = refs
    else:
        h_ref, wt_ref, o_ref, w_ref, acc0_ref, acc1_ref = refs
        acc_refs = (acc0_ref, acc1_ref)
    tm = h_ref.shape[0]
    nsub = tm // sub
    i = pl.program_id(1)

    @pl.when(i == 0)
    def _():
        w_ref[...] = wt_ref[...].T.astype(BF16)

    if conv:
        @pl.when(i % blocks_per_seq == 0)
        def _():
            pad_ref[:, 0:2 * SUBLANES, :] = jnp.zeros((pad_ref.shape[0], 2 * SUBLANES, LANES), F32)

        for r in range(nsub):
            rows = slice(r * sub, (r + 1) * sub)
            acc = _dot(h_ref[rows, :], w_ref[...])
            for c in range(pad_ref.shape[0]):
                cols = slice(c * LANES, (c + 1) * LANES)
                x = acc[:, cols]
                pad_ref[c, pl.ds(2 * SUBLANES, sub, stride=2), :] = x
                y = cb_ref[:, cols] + cw_ref[CONV_WIDTH - 1:CONV_WIDTH, cols] * x
                for s in range(1, CONV_WIDTH):
                    y = y + (cw_ref[CONV_WIDTH - 1 - s:CONV_WIDTH - s, cols]
                             * pad_ref[c, pl.ds(2 * (SUBLANES - s), sub, stride=2), :])
                pad_ref[c, pl.ds(0, SUBLANES, stride=2), :] = x[sub - SUBLANES:, :]
                o_ref[rows, cols] = (_silu(y) if act == "silu" else y).astype(o_ref.dtype)
        return

    for r in range(nsub + 1):
        if r > 0:
            acc = acc_refs[(r - 1) % 2][...]
            o_ref[(r - 1) * sub:r * sub, :] = (_silu(acc) if act == "silu" else _sigmoid(acc)).astype(o_ref.dtype)
        if r < nsub:
            acc_refs[r % 2][...] = _dot(h_ref[r * sub:(r + 1) * sub, :], w_ref[...])


def _proj(h, wt, row0, ncols, tn, seq, name, act=None, conv_w=None, conv_b=None, out_dtype=BF16):
    t, k = h.shape
    conv = conv_w is not None
    tm = _pick(seq, (2048, 1024, 512, 256, 128)) if conv else _pick(seq, (4096, 2048, 1024, 512, 256, 128))
    sub = min(tm, 1024 if conv else 256)
    in_specs = [pl.BlockSpec((tm, k), lambda j, i: (i, 0)),
                pl.BlockSpec((pl.Element(tn), pl.Element(k)),
                             lambda j, i: (pl.multiple_of(row0 + j * tn, SUBLANES), 0))]
    args = [h, wt]
    scratch = [pltpu.VMEM((k, tn), BF16)]
    if conv:
        in_specs += [pl.BlockSpec((CONV_WIDTH, tn), lambda j, i: (0, j)),
                     pl.BlockSpec((1, tn), lambda j, i: (0, j))]
        args += [conv_w, conv_b.reshape(1, ncols)]
        scratch += [pltpu.VMEM((tn // LANES, 2 * (sub + SUBLANES), LANES), F32)]
    else:
        scratch += [pltpu.VMEM((sub, tn), F32), pltpu.VMEM((sub, tn), F32)]
    return pl.pallas_call(
        functools.partial(_proj_kernel, sub=sub, act=act, conv=conv, blocks_per_seq=seq // tm),
        grid=(ncols // tn, t // tm),
        in_specs=in_specs,
        out_specs=pl.BlockSpec((tm, tn), lambda j, i: (i, j)),
        out_shape=jax.ShapeDtypeStruct((t, ncols), out_dtype),
        scratch_shapes=scratch,
        compiler_params=_cparams(("arbitrary", "arbitrary")),
        name=name,
    )(*args)


def _norm_q_dt_kernel(x_ref, g_ref, wqt_ref, wdtt_ref, h_ref, q_ref, dt_ref, wq_ref, wdt_ref, *, sub):
    tm = x_ref.shape[0]

    @pl.when(pl.program_id(0) == 0)
    def _():
        wq_ref[...] = wqt_ref[...].T.astype(BF16)
        wdt_ref[...] = wdtt_ref[...].T.astype(BF16)

    for r in range(tm // sub):
        rows = slice(r * sub, (r + 1) * sub)
        x = x_ref[rows, :]
        ms = jnp.mean(x * x, axis=-1, keepdims=True)
        h = (x * lax.rsqrt(ms + EPS) * g_ref[...]).astype(BF16)
        h_ref[rows, :] = h
        q_ref[rows, :] = _dot(h, wq_ref[...]).astype(q_ref.dtype)
        dt_ref[rows, :] = _dot(h, wdt_ref[...])


def _norm_q_dt(x2d, g, wt, row0, ncols, dt_row0, seq):
    t, k = x2d.shape
    tm = _pick(seq, (1024, 512, 256, 128))
    sub = min(tm, 256)
    return pl.pallas_call(
        functools.partial(_norm_q_dt_kernel, sub=sub),
        grid=(t // tm,),
        in_specs=[pl.BlockSpec((tm, k), lambda i: (i, 0)),
                  pl.BlockSpec((1, k), lambda i: (0, 0)),
                  pl.BlockSpec((pl.Element(ncols), pl.Element(k)), lambda i: (row0, 0)),
                  pl.BlockSpec((pl.Element(LANES), pl.Element(k)), lambda i: (dt_row0, 0))],
        out_specs=[pl.BlockSpec((tm, k), lambda i: (i, 0)),
                   pl.BlockSpec((tm, ncols), lambda i: (i, 0)),
                   pl.BlockSpec((tm, LANES), lambda i: (i, 0))],
        out_shape=[jax.ShapeDtypeStruct((t, k), BF16),
                   jax.ShapeDtypeStruct((t, ncols), BF16),
                   jax.ShapeDtypeStruct((t, LANES), F32)],
        scratch_shapes=[pltpu.VMEM((k, ncols), BF16), pltpu.VMEM((k, LANES), BF16)],
        compiler_params=_cparams(("arbitrary",)),
        name="norm_q_dt",
    )(x2d, g.reshape(1, k), wt, wt)


def _ssd_kernel(zs_ref, xbc_ref, dt_ref, dtb_ref, alog_ref, dexp_ref, ng_ref, e_ref,
                o_ref, state_ref, yacc_ref, rowterm_ref, wgt_ref, acs_ref, eahi_ref, ealo_ref,
                *, width, groups, nstate, heads):
    c = pl.program_id(1)
    L = SSD_CHUNK
    gw = width // groups
    pairs = gw // LANES
    b_off = width
    c_off = width + groups * nstate

    @pl.when(c == 0)
    def _():
        state_ref[...] = jnp.zeros(state_ref.shape, F32)

    row = lax.broadcasted_iota(jnp.int32, (L, L), 0)
    col = lax.broadcasted_iota(jnp.int32, (L, L), 1)
    causal = row >= col
    triu = (row <= col).astype(BF16)
    lo_half = lax.broadcasted_iota(jnp.int32, (L, LANES), 1) < SSD_HEAD_DIM

    nchunks = zs_ref.shape[0] // L
    a_neg = -jnp.exp(alog_ref[...])
    dt_ts = [_softplus(dt_ref[k * L:(k + 1) * L, :].T[:heads, :] + dtb_ref[...]) for k in range(nchunks)]
    parts = [_split3(dt_t * a_neg) for dt_t in dt_ts]
    acs_ts = [sum(_dot(part, triu) for part in p) for p in parts]
    for k in range(nchunks):
        acs_t, dt_t = acs_ts[k], dt_ts[k]
        rowterm_ref[k] = (acs_t - jnp.log(dt_t)) * LOG2E
        wgt_ref[k] = dt_t * jnp.exp(acs_t[:, L - 1:L] - acs_t)
        acs_n = jnp.concatenate([acs_t * LOG2E, jnp.zeros((LANES - heads, L), F32)], axis=0).T
        acs_ref[k] = acs_n
        ea_hi, ea_lo = _split2(jnp.exp2(acs_n))
        eahi_ref[k] = ea_hi
        ealo_ref[k] = ea_lo

    def chunk(ci, carry):
        rows = pl.ds(pl.multiple_of(ci * L, L), L)
        rowterm_t = rowterm_ref[ci]
        wgt_t = wgt_ref[ci]
        acs_n = acs_ref[ci]
        ea_hi = eahi_ref[ci]
        ea_lo = ealo_ref[ci]

        for g in range(groups):
            gcols = slice(g * gw, (g + 1) * gw)
            cg = xbc_ref[rows, c_off + g * nstate:c_off + (g + 1) * nstate]
            bg = xbc_ref[rows, b_off + g * nstate:b_off + (g + 1) * nstate]
            cb = _dot_nt(cg, bg)
            bgt = bg.astype(F32).T
            ea_x = _dot(ea_hi, e_ref[:, gcols]) + _dot(ea_lo, e_ref[:, gcols])
            yacc_ref[g] = _dot(cg, state_ref[g].astype(BF16)) * ea_x
            ssq = jnp.zeros((L, LANES), F32)
            for j in range(pairs):
                h0 = (g * pairs + j) * 2
                tile = slice(j * LANES, (j + 1) * LANES)
                gtile = slice(g * gw + j * LANES, g * gw + (j + 1) * LANES)
                top, bot = [], []
                for h in (h0, h0 + 1):
                    seg = acs_n[:, h:h + 1] - rowterm_t[h:h + 1, :]
                    top.append((jnp.exp2(jnp.where(causal, seg, -jnp.inf)) * cb).astype(BF16))
                    bot.append((bgt * wgt_t[h:h + 1, :]).astype(BF16))
                lhs = jnp.concatenate([jnp.concatenate(top, axis=1), jnp.concatenate(bot, axis=1)],
                                      axis=0)
                xp = xbc_ref[rows, gtile]
                zero = jnp.zeros_like(xp)
                rhs = jnp.concatenate([jnp.where(lo_half, xp, zero), jnp.where(lo_half, zero, xp)], axis=0)
                res = _dot(lhs, rhs)
                state_ref[g, :, tile] = state_ref[g, :, tile] * ea_x[L - 1:L, tile] + res[L:, :]
                y = res[:L, :] + yacc_ref[g, :, tile] + xp.astype(F32) * dexp_ref[:, gtile]
                y = y * zs_ref[rows, gtile].astype(F32)
                yacc_ref[g, :, tile] = y
                ssq = ssq + y * y
            scale = lax.rsqrt(jnp.sum(ssq, axis=-1, keepdims=True) * (1.0 / gw) + EPS)
            o_ref[rows, gcols] = (yacc_ref[g] * scale * ng_ref[:, gcols]).astype(o_ref.dtype)
        return carry

    lax.fori_loop(0, zs_ref.shape[0] // L, chunk, 0, unroll=4)


def _ssd_branch(zs, xbc, dt_raw, dt_bias, a_log, d_skip, norm_g, bsz, seq):
    t, width = zs.shape
    conv_ch = xbc.shape[1]
    heads = dt_bias.shape[0]
    groups = SSD_GROUPS
    nstate = SSD_STATE
    L = SSD_CHUNK
    rows = _pick(seq, (8 * L, 4 * L, 2 * L, L))
    nb = seq // rows
    dtb = jnp.broadcast_to(dt_bias[:, None], (heads, L))
    alog = jnp.broadcast_to(a_log[:, None], (heads, L))
    dexp = jnp.repeat(d_skip, SSD_HEAD_DIM).reshape(1, width)
    expand = (jnp.arange(LANES)[:, None] == (jnp.arange(width) // SSD_HEAD_DIM)[None, :]).astype(BF16)
    row = lambda b, c: (b * nb + c, 0)
    const = lambda b, c: (0, 0)
    return pl.pallas_call(
        functools.partial(_ssd_kernel, width=width, groups=groups, nstate=nstate, heads=heads),
        grid=(bsz, nb),
        in_specs=[pl.BlockSpec((rows, width), row),
                  pl.BlockSpec((rows, conv_ch), row),
                  pl.BlockSpec((rows, LANES), row),
                  pl.BlockSpec((heads, L), const),
                  pl.BlockSpec((heads, L), const),
                  pl.BlockSpec((1, width), const),
                  pl.BlockSpec((1, width), const),
                  pl.BlockSpec((LANES, width), const)],
        out_specs=pl.BlockSpec((rows, width), row),
        out_shape=jax.ShapeDtypeStruct((t, width), BF16),
        scratch_shapes=[pltpu.VMEM((groups, nstate, width // groups), F32),
                        pltpu.VMEM((groups, L, width // groups), F32),
                        pltpu.VMEM((rows // L, heads, L), F32),
                        pltpu.VMEM((rows // L, heads, L), F32),
                        pltpu.VMEM((rows // L, L, LANES), F32),
                        pltpu.VMEM((rows // L, L, LANES), BF16),
                        pltpu.VMEM((rows // L, L, LANES), BF16)],
        compiler_params=_cparams(("arbitrary", "arbitrary")),
        name="ssd",
    )(zs, xbc, dt_raw, dtb, alog, dexp, norm_g.reshape(1, width), expand)


def _lru_kernel(slg_ref, xl_ref, wg_ref, ba_ref, bx_ref, lam_ref, o_ref, a_ref, u_ref, carry_ref, *, rt):
    c = pl.program_id(1)
    rows, width = xl_ref.shape
    nt = width // LANES
    tg = width // LRU_TILE_GROUPS
    tiles_per_group = tg // LANES

    @pl.when(c == 0)
    def _():
        carry_ref[...] = jnp.zeros(carry_ref.shape, F32)

    def gate_tile(ti, carry):
        rs = pl.ds(pl.multiple_of(ti * rt, rt), rt)
        for g in range(LRU_TILE_GROUPS):
            gs = slice(g * tg, (g + 1) * tg)
            xl = xl_ref[rs, gs]
            gg = _dot(xl, wg_ref[g])
            tr = jnp.tanh(gg[:, :tg] + ba_ref[:, gs])
            ti = jnp.tanh(gg[:, tg:] + bx_ref[:, gs])
            ch = (-0.5 * LRU_C * LOG2E) * _softplus(-lam_ref[:, gs])
            a = jnp.exp2(tr * ch + ch)
            u = _sqrt_nonneg(1.0 - a * a) * ((0.5 * ti + 0.5) * xl.astype(F32))
            for k in range(tiles_per_group):
                a_ref[g * tiles_per_group + k, rs, :] = a[:, k * LANES:(k + 1) * LANES]
                u_ref[g * tiles_per_group + k, rs, :] = u[:, k * LANES:(k + 1) * LANES]
        return carry

    lax.fori_loop(0, rows // rt, gate_tile, 0, unroll=4)

    sub = lax.broadcasted_iota(jnp.int32, (SUBLANES, LANES), 0)

    def mini(m, carry):
        base = pl.multiple_of(m * SCAN_ROWS, SCAN_ROWS)
        for t in range(nt):
            cprev = carry_ref[:, t * LANES:(t + 1) * LANES]
            av = [a_ref[t, pl.ds(base + j, SUBLANES, stride=SCAN_STRIDE), :] for j in range(SCAN_STRIDE)]
            uv = [u_ref[t, pl.ds(base + j, SUBLANES, stride=SCAN_STRIDE), :] for j in range(SCAN_STRIDE)]
            h, p = uv[0], av[0]
            for j in range(1, SCAN_STRIDE):
                h = av[j] * h + uv[j]
                p = p * av[j]
            d = 1
            while d < SUBLANES:
                hs = pltpu.roll(h, d, 0)
                ps = pltpu.roll(p, d, 0)
                keep = sub < d
                h = jnp.where(keep, h, p * hs + h)
                p = jnp.where(keep, p, p * ps)
                d *= 2
            e = p * cprev + h
            hh = jnp.where(sub == 0, cprev, pltpu.roll(e, 1, 0))
            for j in range(SCAN_STRIDE):
                hh = av[j] * hh + uv[j]
                u_ref[t, pl.ds(base + j, SUBLANES, stride=SCAN_STRIDE), :] = hh
            carry_ref[:, t * LANES:(t + 1) * LANES] = jnp.broadcast_to(e[SUBLANES - 1:SUBLANES, :],
                                                                       (SUBLANES, LANES))
        return carry

    lax.fori_loop(0, rows // SCAN_ROWS, mini, 0, unroll=4)

    for t in range(nt):
        ts = slice(t * LANES, (t + 1) * LANES)
        o_ref[:, ts] = (u_ref[t] * slg_ref[:, ts].astype(F32)).astype(o_ref.dtype)


def _lru_branch(slg, xl, w_a, b_a, w_x, b_x, lam, bsz, seq):
    t, width = xl.shape
    rows = _pick(seq, (1024, 512, 256, 128))
    nb = seq // rows
    nblk, blk = w_a.shape[0], w_a.shape[1]
    per = nblk // LRU_TILE_GROUPS
    tg = width // LRU_TILE_GROUPS

    def tile_diag(w):
        wt = w.reshape(LRU_TILE_GROUPS, per, blk, blk)
        eye = jnp.eye(per, dtype=w.dtype)
        return jnp.einsum("gpij,pq->gpiqj", wt, eye).reshape(LRU_TILE_GROUPS, tg, tg)

    wg = (0.5 * jnp.concatenate([tile_diag(w_a), tile_diag(w_x)], axis=2)).astype(BF16)
    b_a, b_x = 0.5 * b_a, 0.5 * b_x
    row = lambda b, c: (b * nb + c, 0)
    const = lambda b, c: (0, 0)
    return pl.pallas_call(
        functools.partial(_lru_kernel, rt=min(rows, 128)),
        grid=(bsz, nb),
        in_specs=[pl.BlockSpec((rows, width), row),
                  pl.BlockSpec((rows, width), row),
                  pl.BlockSpec((LRU_TILE_GROUPS, tg, 2 * tg), lambda b, c: (0, 0, 0)),
                  pl.BlockSpec((1, width), const),
                  pl.BlockSpec((1, width), const),
                  pl.BlockSpec((1, width), const)],
        out_specs=pl.BlockSpec((rows, width), row),
        out_shape=jax.ShapeDtypeStruct((t, width), BF16),
        scratch_shapes=[pltpu.VMEM((width // LANES, rows, LANES), F32),
                        pltpu.VMEM((width // LANES, rows, LANES), F32),
                        pltpu.VMEM((SUBLANES, width), F32)],
        compiler_params=_cparams(("arbitrary", "arbitrary")),
        name="lru",
    )(slg, xl, wg, b_a.reshape(1, width), b_x.reshape(1, width), lam.reshape(1, width))


def _kv_kernel(m_ref, g_ref, w_ref, o_ref):
    m = m_ref[...]
    ms = jnp.mean(m * m, axis=-1, keepdims=True)
    mn = (m * lax.rsqrt(ms + EPS) * g_ref[...]).astype(BF16)
    o_ref[...] = _dot(mn, w_ref[...]).astype(o_ref.dtype)


def _mem_kv(mem2d, g, w_kv_b, tn):
    m, d = mem2d.shape
    n = w_kv_b.shape[1]
    return pl.pallas_call(
        _kv_kernel,
        grid=(n // tn,),
        in_specs=[pl.BlockSpec((m, d), lambda j: (0, 0)),
                  pl.BlockSpec((1, d), lambda j: (0, 0)),
                  pl.BlockSpec((d, tn), lambda j: (0, j))],
        out_specs=pl.BlockSpec((m, tn), lambda j: (0, j)),
        out_shape=jax.ShapeDtypeStruct((m, n), BF16),
        compiler_params=_cparams(("parallel",)),
        name="mem_kv",
    )(mem2d, g.reshape(1, d), w_kv_b)


def _merge_kernel(ys_ref, yl_ref, q_ref, kv_ref, gt_ref, x_ref, ws_ref, wl_ref, wm_ref, wo_ref, fg_ref, o_ref,
                  ym_ref, *, final_norm):
    d = x_ref.shape[1]
    hd = d // MEM_HEADS
    scale = hd ** -0.5
    sub = min(q_ref.shape[0], 256)
    for bi in range(q_ref.shape[0] // sub):
        rows = slice(bi * sub, (bi + 1) * sub)
        for h in range(MEM_HEADS):
            hs = slice(h * hd, (h + 1) * hd)
            s = _dot_nt(q_ref[rows, hs], kv_ref[:, hs]) * scale
            p = jnp.exp(s - jnp.max(s, axis=-1, keepdims=True))
            l = jnp.sum(p, axis=-1, keepdims=True)
            o = _dot(p.astype(BF16), kv_ref[:, d + h * hd:d + (h + 1) * hd]) / l
            ym_ref[rows, hs] = o.astype(ym_ref.dtype)
    merged = (gt_ref[:, :d].astype(F32) * _dot(ys_ref[...], ws_ref[...])
              + gt_ref[:, d:2 * d].astype(F32) * _dot(yl_ref[...], wl_ref[...])
              + gt_ref[:, 2 * d:].astype(F32) * _dot(ym_ref[...], wm_ref[...]))
    xn = x_ref[...] + _dot(merged.astype(BF16), wo_ref[...])
    if final_norm:
        ms = jnp.mean(xn * xn, axis=-1, keepdims=True)
        xn = xn * lax.rsqrt(ms + EPS) * fg_ref[...]
    o_ref[...] = xn


def _merge(y_ssd, y_lru, q, kv, gates, x2d, w_s, w_l, w_m, w_o, final_g, final_norm, seq):
    t, d = x2d.shape
    tm = _pick(seq, (512, 256, 128))
    blocks_per_seq = seq // tm
    m = kv.shape[0] // (t // seq)
    row = lambda i: (i, 0)
    const = lambda i: (0, 0)
    full = lambda a: pl.BlockSpec(a.shape, const)
    return pl.pallas_call(
        functools.partial(_merge_kernel, final_norm=final_norm),
        grid=(t // tm,),
        in_specs=[pl.BlockSpec((tm, y_ssd.shape[1]), row),
                  pl.BlockSpec((tm, y_lru.shape[1]), row),
                  pl.BlockSpec((tm, d), row),
                  pl.BlockSpec((m, 2 * d), lambda i: (i // blocks_per_seq, 0)),
                  pl.BlockSpec((tm, gates.shape[1]), row),
                  pl.BlockSpec((tm, d), row),
                  full(w_s), full(w_l), full(w_m), full(w_o),
                  pl.BlockSpec((1, d), const)],
        out_specs=pl.BlockSpec((tm, d), row),
        out_shape=jax.ShapeDtypeStruct((t, d), F32),
        scratch_shapes=[pltpu.VMEM((tm, d), BF16)],
        compiler_params=_cparams(("parallel",)),
        name="merge",
    )(y_ssd, y_lru, q, kv, gates, x2d, w_s, w_l, w_m, w_o, final_g.reshape(1, d))


def kernel(x, mem, norm_g, w_in, ssd_conv_w, ssd_conv_b, ssd_dt_bias, ssd_a_log, ssd_d, ssd_norm_g, lru_conv_w, lru_conv_b, lru_w_a, lru_b_a, lru_w_x, lru_b_x, lru_lambda, mem_norm_g, w_kv, w_br_ssd, w_br_lru, w_br_mem, w_out, final_g):
    bsz, seq, d = x.shape
    depth = norm_g.shape[0]
    t = bsz * seq
    heads = ssd_dt_bias.shape[1]
    ssd_w = heads * SSD_HEAD_DIM
    conv_ch = ssd_conv_w.shape[2]
    lru_w = lru_conv_w.shape[2]
    x2d = x.reshape(t, d)
    mem2d = mem.reshape(bsz * mem.shape[1], d)
    front = ssd_w + conv_ch
    tn = lambda n: _pick(n, (1024, 768, 512, 256, 128))
    for l in range(depth):
        wt = jnp.swapaxes(w_in[l], 0, 1)
        rest = front + heads

        h, q, dt_raw = _norm_q_dt(x2d, norm_g[l], wt, rest + 2 * lru_w, d, front, seq)
        zs = _proj(h, wt, 0, ssd_w, tn(ssd_w), seq, "proj_z", act="silu")
        xbc = _proj(h, wt, ssd_w, conv_ch, tn(conv_ch), seq, "proj_xbc", act="silu",
                    conv_w=ssd_conv_w[l], conv_b=ssd_conv_b[l])
        slg = _proj(h, wt, rest, lru_w, tn(lru_w), seq, "proj_lg", act="silu")
        xl = _proj(h, wt, rest + lru_w, lru_w, tn(lru_w), seq, "proj_lx",
                   conv_w=lru_conv_w[l], conv_b=lru_conv_b[l])
        gates = _proj(h, wt, rest + 2 * lru_w + d, 3 * d, tn(3 * d), seq, "proj_gates", act="sigmoid")

        y_ssd = _ssd_branch(zs, xbc, dt_raw, ssd_dt_bias[l], ssd_a_log[l], ssd_d[l],
                            ssd_norm_g[l].reshape(-1), bsz, seq)
        y_lru = _lru_branch(slg, xl, lru_w_a[l], lru_b_a[l].reshape(-1), lru_w_x[l], lru_b_x[l].reshape(-1),
                            lru_lambda[l], bsz, seq)
        kv = _mem_kv(mem2d, mem_norm_g[l], w_kv[l].astype(BF16), _pick(2 * d, (512, 256, 128)))
        x2d = _merge(y_ssd, y_lru, q, kv, gates, x2d, w_br_ssd[l].astype(BF16), w_br_lru[l].astype(BF16),
                     w_br_mem[l].astype(BF16), w_out[l].astype(BF16), final_g, l == depth - 1, seq)
    return x2d.reshape(bsz, seq, d)
```

```python
import functools

import jax
import jax.numpy as jnp
from jax import lax
from jax.experimental import pallas as pl
from jax.experimental.pallas import tpu as pltpu

F32 = jnp.float32
BF16 = jnp.bfloat16

EPS = 1e-6
CONV_WIDTH = 4
SSD_HEAD_DIM = 64
SSD_GROUPS = 4
SSD_STATE = 128
SSD_CHUNK = 128
LRU_C = 8.0
MEM_HEADS = 4
LANES = 128
SUBLANES = 8
LRU_TILE_GROUPS = 4
SCAN_STRIDE = 4
SCAN_ROWS = SCAN_STRIDE * SUBLANES

VMEM_LIMIT = 56 * 1024 * 1024


def _cparams(sem):
    return pltpu.CompilerParams(dimension_semantics=sem, vmem_limit_bytes=VMEM_LIMIT)


LOG2E = 1.4426950408889634
TINY = 1e-30


def _sigmoid(x):
    return 1.0 / (1.0 + jnp.exp2(x * (-LOG2E)))


def _sqrt_nonneg(x):
    return x * lax.rsqrt(jnp.maximum(x, TINY))


def _silu(x):
    return x * _sigmoid(x)


def _softplus(x):
    return jnp.maximum(x, 0.0) + jnp.log1p(jnp.exp(-jnp.abs(x)))


def _split2(x):
    hi = x.astype(BF16)
    lo = (x - hi.astype(F32)).astype(BF16)
    return hi, lo


def _split3(x):
    hi = x.astype(BF16)
    r = x - hi.astype(F32)
    mid = r.astype(BF16)
    lo = (r - mid.astype(F32)).astype(BF16)
    return hi, mid, lo


def _dot(a, b):
    return jnp.dot(a, b, preferred_element_type=F32)


def _dot_nt(a, b):
    return lax.dot_general(a, b, (((1,), (1,)), ((), ())), preferred_element_type=F32)


def _pick(n, prefs):
    for p in prefs:
        if n % p == 0:
            return p
    return n


def _proj_kernel(*refs, sub, act, conv, blocks_per_seq):
    if conv:
        h_ref, wt_ref, cw_ref, cb_ref, o_ref, w_ref, pad_ref = refs
    else:
        h_ref, wt_ref, o_ref, w_ref, acc0_ref, acc1_ref = refs
        acc_refs = (acc0_ref, acc1_ref)
    tm = h_ref.shape[0]
    nsub = tm // sub
    i = pl.program_id(1)

    @pl.when(i == 0)
    def _():
        w_ref[...] = wt_ref[...].T.astype(BF16)

    if conv:
        @pl.when(i % blocks_per_seq == 0)
        def _():
            pad_ref[:, 0:2 * SUBLANES, :] = jnp.zeros((pad_ref.shape[0], 2 * SUBLANES, LANES), F32)

        for r in range(nsub):
            rows = slice(r * sub, (r + 1) * sub)
            acc = _dot(h_ref[rows, :], w_ref[...])
            for c in range(pad_ref.shape[0]):
                cols = slice(c * LANES, (c + 1) * LANES)
                x = acc[:, cols]
                pad_ref[c, pl.ds(2 * SUBLANES, sub, stride=2), :] = x
                y = cb_ref[:, cols] + cw_ref[CONV_WIDTH - 1:CONV_WIDTH, cols] * x
                for s in range(1, CONV_WIDTH):
                    y = y + (cw_ref[CONV_WIDTH - 1 - s:CONV_WIDTH - s, cols]
                             * pad_ref[c, pl.ds(2 * (SUBLANES - s), sub, stride=2), :])
                pad_ref[c, pl.ds(0, SUBLANES, stride=2), :] = x[sub - SUBLANES:, :]
                o_ref[rows, cols] = (_silu(y) if act == "silu" else y).astype(o_ref.dtype)
        return

    for r in range(nsub + 1):
        if r > 0:
            acc = acc_refs[(r - 1) % 2][...]
            o_ref[(r - 1) * sub:r * sub, :] = (_silu(acc) if act == "silu" else _sigmoid(acc)).astype(o_ref.dtype)
        if r < nsub:
            acc_refs[r % 2][...] = _dot(h_ref[r * sub:(r + 1) * sub, :], w_ref[...])


def _proj(h, wt, row0, ncols, tn, seq, name, act=None, conv_w=None, conv_b=None, out_dtype=BF16):
    t, k = h.shape
    conv = conv_w is not None
    tall = not conv and tn >= 1024
    tm = _pick(seq, (4096, 2048, 1024, 512, 256, 128)) if tall else _pick(seq, (2048, 1024, 512, 256, 128))
    sub = min(tm, 1024 if conv else 256)
    in_specs = [pl.BlockSpec((tm, k), lambda j, i: (i, 0)),
                pl.BlockSpec((pl.Element(tn), pl.Element(k)),
                             lambda j, i: (pl.multiple_of(row0 + j * tn, SUBLANES), 0))]
    args = [h, wt]
    scratch = [pltpu.VMEM((k, tn), BF16)]
    if conv:
        in_specs += [pl.BlockSpec((CONV_WIDTH, tn), lambda j, i: (0, j)),
                     pl.BlockSpec((1, tn), lambda j, i: (0, j))]
        args += [conv_w, conv_b.reshape(1, ncols)]
        scratch += [pltpu.VMEM((tn // LANES, 2 * (sub + SUBLANES), LANES), F32)]
    else:
        scratch += [pltpu.VMEM((sub, tn), F32), pltpu.VMEM((sub, tn), F32)]
    return pl.pallas_call(
        functools.partial(_proj_kernel, sub=sub, act=act, conv=conv, blocks_per_seq=seq // tm),
        grid=(ncols // tn, t // tm),
        in_specs=in_specs,
        out_specs=pl.BlockSpec((tm, tn), lambda j, i: (i, j)),
        out_shape=jax.ShapeDtypeStruct((t, ncols), out_dtype),
        scratch_shapes=scratch,
        compiler_params=_cparams(("arbitrary", "arbitrary")),
        name=name,
    )(*args)


def _norm_q_dt_kernel(x_ref, g_ref, wqt_ref, wdtt_ref, h_ref, q_ref, dt_ref, wq_ref, wdt_ref, *, sub):
    tm = x_ref.shape[0]

    @pl.when(pl.program_id(0) == 0)
    def _():
        wq_ref[...] = wqt_ref[...].T.astype(BF16)
        wdt_ref[...] = wdtt_ref[...].T.astype(BF16)

    for r in range(tm // sub):
        rows = slice(r * sub, (r + 1) * sub)
        x = x_ref[rows, :]
        ms = jnp.mean(x * x, axis=-1, keepdims=True)
        h = (x * lax.rsqrt(ms + EPS) * g_ref[...]).astype(BF16)
        h_ref[rows, :] = h
        q_ref[rows, :] = _dot(h, wq_ref[...]).astype(q_ref.dtype)
        dt_ref[rows, :] = _dot(h, wdt_ref[...])


def _norm_q_dt(x2d, g, wt, row0, ncols, dt_row0, seq):
    t, k = x2d.shape
    tm = _pick(seq, (1024, 512, 256, 128))
    sub = min(tm, 256)
    return pl.pallas_call(
        functools.partial(_norm_q_dt_kernel, sub=sub),
        grid=(t // tm,),
        in_specs=[pl.BlockSpec((tm, k), lambda i: (i, 0)),
                  pl.BlockSpec((1, k), lambda i: (0, 0)),
                  pl.BlockSpec((pl.Element(ncols), pl.Element(k)), lambda i: (row0, 0)),
                  pl.BlockSpec((pl.Element(LANES), pl.Element(k)), lambda i: (dt_row0, 0))],
        out_specs=[pl.BlockSpec((tm, k), lambda i: (i, 0)),
                   pl.BlockSpec((tm, ncols), lambda i: (i, 0)),
                   pl.BlockSpec((tm, LANES), lambda i: (i, 0))],
        out_shape=[jax.ShapeDtypeStruct((t, k), BF16),
                   jax.ShapeDtypeStruct((t, ncols), BF16),
                   jax.ShapeDtypeStruct((t, LANES), F32)],
        scratch_shapes=[pltpu.VMEM((k, ncols), BF16), pltpu.VMEM((k, LANES), BF16)],
        compiler_params=_cparams(("arbitrary",)),
        name="norm_q_dt",
    )(x2d, g.reshape(1, k), wt, wt)


def _ssd_kernel(zs_ref, xbc_ref, dt_ref, dtb_ref, alog_ref, dexp_ref, ng_ref, e_ref,
                o_ref, state_ref, yacc_ref, rowterm_ref, wgt_ref, acs_ref, eahi_ref, ealo_ref,
                *, width, groups, nstate, heads):
    c = pl.program_id(1)
    L = SSD_CHUNK
    gw = width // groups
    pairs = gw // LANES
    b_off = width
    c_off = width + groups * nstate

    @pl.when(c == 0)
    def _():
        state_ref[...] = jnp.zeros(state_ref.shape, F32)

    row = lax.broadcasted_iota(jnp.int32, (L, L), 0)
    col = lax.broadcasted_iota(jnp.int32, (L, L), 1)
    causal = row >= col
    triu = (row <= col).astype(BF16)
    lo_half = lax.broadcasted_iota(jnp.int32, (L, LANES), 1) < SSD_HEAD_DIM

    nchunks = zs_ref.shape[0] // L
    a_neg = -jnp.exp(alog_ref[...])
    dt_ts = [_softplus(dt_ref[k * L:(k + 1) * L, :].T[:heads, :] + dtb_ref[...]) for k in range(nchunks)]
    parts = [_split3(dt_t * a_neg) for dt_t in dt_ts]
    acs_ts = [sum(_dot(part, triu) for part in p) for p in parts]
    for k in range(nchunks):
        acs_t, dt_t = acs_ts[k], dt_ts[k]
        rowterm_ref[k] = (acs_t - jnp.log(dt_t)) * LOG2E
        wgt_ref[k] = dt_t * jnp.exp(acs_t[:, L - 1:L] - acs_t)
        acs_n = jnp.concatenate([acs_t * LOG2E, jnp.zeros((LANES - heads, L), F32)], axis=0).T
        acs_ref[k] = acs_n
        ea_hi, ea_lo = _split2(jnp.exp2(acs_n))
        eahi_ref[k] = ea_hi
        ealo_ref[k] = ea_lo

    def chunk(ci, carry):
        rows = pl.ds(pl.multiple_of(ci * L, L), L)
        rowterm_t = rowterm_ref[ci]
        wgt_t = wgt_ref[ci]
        acs_n = acs_ref[ci]
        ea_hi = eahi_ref[ci]
        ea_lo = ealo_ref[ci]

        for g in range(groups):
            gcols = slice(g * gw, (g + 1) * gw)
            cg = xbc_ref[rows, c_off + g * nstate:c_off + (g + 1) * nstate]
            bg = xbc_ref[rows, b_off + g * nstate:b_off + (g + 1) * nstate]
            cb = _dot_nt(cg, bg)
            bgt = bg.astype(F32).T
            ea_x = _dot(ea_hi, e_ref[:, gcols]) + _dot(ea_lo, e_ref[:, gcols])
            yacc_ref[g] = _dot(cg, state_ref[g].astype(BF16)) * ea_x
            ssq = jnp.zeros((L, LANES), F32)
            for j in range(pairs):
                h0 = (g * pairs + j) * 2
                tile = slice(j * LANES, (j + 1) * LANES)
                gtile = slice(g * gw + j * LANES, g * gw + (j + 1) * LANES)
                top, bot = [], []
                for h in (h0, h0 + 1):
                    seg = acs_n[:, h:h + 1] - rowterm_t[h:h + 1, :]
                    top.append((jnp.exp2(jnp.where(causal, seg, -jnp.inf)) * cb).astype(BF16))
                    bot.append((bgt * wgt_t[h:h + 1, :]).astype(BF16))
                lhs = jnp.concatenate([jnp.concatenate(top, axis=1), jnp.concatenate(bot, axis=1)],
                                      axis=0)
                xp = xbc_ref[rows, gtile]
                zero = jnp.zeros_like(xp)
                rhs = jnp.concatenate([jnp.where(lo_half, xp, zero), jnp.where(lo_half, zero, xp)], axis=0)
                res = _dot(lhs, rhs)
                state_ref[g, :, tile] = state_ref[g, :, tile] * ea_x[L - 1:L, tile] + res[L:, :]
                y = res[:L, :] + yacc_ref[g, :, tile] + xp.astype(F32) * dexp_ref[:, gtile]
                y = y * zs_ref[rows, gtile].astype(F32)
                yacc_ref[g, :, tile] = y
                ssq = ssq + y * y
            scale = lax.rsqrt(jnp.sum(ssq, axis=-1, keepdims=True) * (1.0 / gw) + EPS)
            o_ref[rows, gcols] = (yacc_ref[g] * scale * ng_ref[:, gcols]).astype(o_ref.dtype)
        return carry

    lax.fori_loop(0, zs_ref.shape[0] // L, chunk, 0, unroll=4)


def _ssd_branch(zs, xbc, dt_raw, dt_bias, a_log, d_skip, norm_g, bsz, seq):
    t, width = zs.shape
    conv_ch = xbc.shape[1]
    heads = dt_bias.shape[0]
    groups = SSD_GROUPS
    nstate = SSD_STATE
    L = SSD_CHUNK
    rows = _pick(seq, (8 * L, 4 * L, 2 * L, L))
    nb = seq // rows
    dtb = jnp.broadcast_to(dt_bias[:, None], (heads, L))
    alog = jnp.broadcast_to(a_log[:, None], (heads, L))
    dexp = jnp.repeat(d_skip, SSD_HEAD_DIM).reshape(1, width)
    expand = (jnp.arange(LANES)[:, None] == (jnp.arange(width) // SSD_HEAD_DIM)[None, :]).astype(BF16)
    row = lambda b, c: (b * nb + c, 0)
    const = lambda b, c: (0, 0)
    return pl.pallas_call(
        functools.partial(_ssd_kernel, width=width, groups=groups, nstate=nstate, heads=heads),
        grid=(bsz, nb),
        in_specs=[pl.BlockSpec((rows, width), row),
                  pl.BlockSpec((rows, conv_ch), row),
                  pl.BlockSpec((rows, LANES), row),
                  pl.BlockSpec((heads, L), const),
                  pl.BlockSpec((heads, L), const),
                  pl.BlockSpec((1, width), const),
                  pl.BlockSpec((1, width), const),
                  pl.BlockSpec((LANES, width), const)],
        out_specs=pl.BlockSpec((rows, width), row),
        out_shape=jax.ShapeDtypeStruct((t, width), BF16),
        scratch_shapes=[pltpu.VMEM((groups, nstate, width // groups), F32),
                        pltpu.VMEM((groups, L, width // groups), F32),
                        pltpu.VMEM((rows // L, heads, L), F32),
                        pltpu.VMEM((rows // L, heads, L), F32),
                        pltpu.VMEM((rows // L, L, LANES), F32),
                        pltpu.VMEM((rows // L, L, LANES), BF16),
                        pltpu.VMEM((rows // L, L, LANES), BF16)],
        compiler_params=_cparams(("arbitrary", "arbitrary")),
        name="ssd",
    )(zs, xbc, dt_raw, dtb, alog, dexp, norm_g.reshape(1, width), expand)


def _lru_kernel(slg_ref, xl_ref, wg_ref, ba_ref, bx_ref, lam_ref, o_ref, a_ref, u_ref, carry_ref, *, rt):
    c = pl.program_id(1)
    rows, width = xl_ref.shape
    nt = width // LANES
    tg = width // LRU_TILE_GROUPS
    tiles_per_group = tg // LANES

    @pl.when(c == 0)
    def _():
        carry_ref[...] = jnp.zeros(carry_ref.shape, F32)

    def gate_tile(ti, carry):
        rs = pl.ds(pl.multiple_of(ti * rt, rt), rt)
        for g in range(LRU_TILE_GROUPS):
            gs = slice(g * tg, (g + 1) * tg)
            xl = xl_ref[rs, gs]
            gg = _dot(xl, wg_ref[g])
            tr = jnp.tanh(gg[:, :tg] + ba_ref[:, gs])
            ti = jnp.tanh(gg[:, tg:] + bx_ref[:, gs])
            ch = (-0.5 * LRU_C * LOG2E) * _softplus(-lam_ref[:, gs])
            a = jnp.exp2(tr * ch + ch)
            u = _sqrt_nonneg(1.0 - a * a) * ((0.5 * ti + 0.5) * xl.astype(F32))
            for k in range(tiles_per_group):
                a_ref[g * tiles_per_group + k, rs, :] = a[:, k * LANES:(k + 1) * LANES]
                u_ref[g * tiles_per_group + k, rs, :] = u[:, k * LANES:(k + 1) * LANES]
        return carry

    lax.fori_loop(0, rows // rt, gate_tile, 0, unroll=4)

    sub = lax.broadcasted_iota(jnp.int32, (SUBLANES, LANES), 0)

    def mini(m, carry):
        base = pl.multiple_of(m * SCAN_ROWS, SCAN_ROWS)
        for t in range(nt):
            cprev = carry_ref[:, t * LANES:(t + 1) * LANES]
            av = [a_ref[t, pl.ds(base + j, SUBLANES, stride=SCAN_STRIDE), :] for j in range(SCAN_STRIDE)]
            uv = [u_ref[t, pl.ds(base + j, SUBLANES, stride=SCAN_STRIDE), :] for j in range(SCAN_STRIDE)]
            h, p = uv[0], av[0]
            for j in range(1, SCAN_STRIDE):
                h = av[j] * h + uv[j]
                p = p * av[j]
            d = 1
            while d < SUBLANES:
                hs = pltpu.roll(h, d, 0)
                ps = pltpu.roll(p, d, 0)
                keep = sub < d
                h = jnp.where(keep, h, p * hs + h)
                p = jnp.where(keep, p, p * ps)
                d *= 2
            e = p * cprev + h
            hh = jnp.where(sub == 0, cprev, pltpu.roll(e, 1, 0))
            for j in range(SCAN_STRIDE):
                hh = av[j] * hh + uv[j]
                u_ref[t, pl.ds(base + j, SUBLANES, stride=SCAN_STRIDE), :] = hh
            carry_ref[:, t * LANES:(t + 1) * LANES] = jnp.broadcast_to(e[SUBLANES - 1:SUBLANES, :],
                                                                       (SUBLANES, LANES))
        return carry

    lax.fori_loop(0, rows // SCAN_ROWS, mini, 0, unroll=4)

    for t in range(nt):
        ts = slice(t * LANES, (t + 1) * LANES)
        o_ref[:, ts] = (u_ref[t] * slg_ref[:, ts].astype(F32)).astype(o_ref.dtype)


def _lru_branch(slg, xl, w_a, b_a, w_x, b_x, lam, bsz, seq):
    t, width = xl.shape
    rows = _pick(seq, (1024, 512, 256, 128))
    nb = seq // rows
    nblk, blk = w_a.shape[0], w_a.shape[1]
    per = nblk // LRU_TILE_GROUPS
    tg = width // LRU_TILE_GROUPS

    def tile_diag(w):
        wt = w.reshape(LRU_TILE_GROUPS, per, blk, blk)
        eye = jnp.eye(per, dtype=w.dtype)
        return jnp.einsum("gpij,pq->gpiqj", wt, eye).reshape(LRU_TILE_GROUPS, tg, tg)

    wg = (0.5 * jnp.concatenate([tile_diag(w_a), tile_diag(w_x)], axis=2)).astype(BF16)
    b_a, b_x = 0.5 * b_a, 0.5 * b_x
    row = lambda b, c: (b * nb + c, 0)
    const = lambda b, c: (0, 0)
    return pl.pallas_call(
        functools.partial(_lru_kernel, rt=min(rows, 128)),
        grid=(bsz, nb),
        in_specs=[pl.BlockSpec((rows, width), row),
                  pl.BlockSpec((rows, width), row),
                  pl.BlockSpec((LRU_TILE_GROUPS, tg, 2 * tg), lambda b, c: (0, 0, 0)),
                  pl.BlockSpec((1, width), const),
                  pl.BlockSpec((1, width), const),
                  pl.BlockSpec((1, width), const)],
        out_specs=pl.BlockSpec((rows, width), row),
        out_shape=jax.ShapeDtypeStruct((t, width), BF16),
        scratch_shapes=[pltpu.VMEM((width // LANES, rows, LANES), F32),
                        pltpu.VMEM((width // LANES, rows, LANES), F32),
                        pltpu.VMEM((SUBLANES, width), F32)],
        compiler_params=_cparams(("arbitrary", "arbitrary")),
        name="lru",
    )(slg, xl, wg, b_a.reshape(1, width), b_x.reshape(1, width), lam.reshape(1, width))


def _kv_kernel(m_ref, g_ref, w_ref, o_ref):
    m = m_ref[...]
    ms = jnp.mean(m * m, axis=-1, keepdims=True)
    mn = (m * lax.rsqrt(ms + EPS) * g_ref[...]).astype(BF16)
    o_ref[...] = _dot(mn, w_ref[...]).astype(o_ref.dtype)


def _mem_kv(mem2d, g, w_kv_b, tn):
    m, d = mem2d.shape
    n = w_kv_b.shape[1]
    return pl.pallas_call(
        _kv_kernel,
        grid=(n // tn,),
        in_specs=[pl.BlockSpec((m, d), lambda j: (0, 0)),
                  pl.BlockSpec((1, d), lambda j: (0, 0)),
                  pl.BlockSpec((d, tn), lambda j: (0, j))],
        out_specs=pl.BlockSpec((m, tn), lambda j: (0, j)),
        out_shape=jax.ShapeDtypeStruct((m, n), BF16),
        compiler_params=_cparams(("parallel",)),
        name="mem_kv",
    )(mem2d, g.reshape(1, d), w_kv_b)


def _merge_kernel(ys_ref, yl_ref, q_ref, kv_ref, gt_ref, x_ref, ws_ref, wl_ref, wm_ref, wo_ref, fg_ref, o_ref,
                  ym_ref, *, final_norm):
    d = x_ref.shape[1]
    hd = d // MEM_HEADS
    scale = hd ** -0.5
    sub = min(q_ref.shape[0], 256)
    for bi in range(q_ref.shape[0] // sub):
        rows = slice(bi * sub, (bi + 1) * sub)
        for h in range(MEM_HEADS):
            hs = slice(h * hd, (h + 1) * hd)
            s = _dot_nt(q_ref[rows, hs], kv_ref[:, hs]) * scale
            p = jnp.exp(s - jnp.max(s, axis=-1, keepdims=True))
            l = jnp.sum(p, axis=-1, keepdims=True)
            o = _dot(p.astype(BF16), kv_ref[:, d + h * hd:d + (h + 1) * hd]) / l
            ym_ref[rows, hs] = o.astype(ym_ref.dtype)
    merged = (gt_ref[:, :d].astype(F32) * _dot(ys_ref[...], ws_ref[...])
              + gt_ref[:, d:2 * d].astype(F32) * _dot(yl_ref[...], wl_ref[...])
              + gt_ref[:, 2 * d:].astype(F32) * _dot(ym_ref[...], wm_ref[...]))
    xn = x_ref[...] + _dot(merged.astype(BF16), wo_ref[...])
    if final_norm:
        ms = jnp.mean(xn * xn, axis=-1, keepdims=True)
        xn = xn * lax.rsqrt(ms + EPS) * fg_ref[...]
    o_ref[...] = xn


def _merge(y_ssd, y_lru, q, kv, gates, x2d, w_s, w_l, w_m, w_o, final_g, final_norm, seq):
    t, d = x2d.shape
    tm = _pick(seq, (512, 256, 128))
    blocks_per_seq = seq // tm
    m = kv.shape[0] // (t // seq)
    row = lambda i: (i, 0)
    const = lambda i: (0, 0)
    full = lambda a: pl.BlockSpec(a.shape, const)
    return pl.pallas_call(
        functools.partial(_merge_kernel, final_norm=final_norm),
        grid=(t // tm,),
        in_specs=[pl.BlockSpec((tm, y_ssd.shape[1]), row),
                  pl.BlockSpec((tm, y_lru.shape[1]), row),
                  pl.BlockSpec((tm, d), row),
                  pl.BlockSpec((m, 2 * d), lambda i: (i // blocks_per_seq, 0)),
                  pl.BlockSpec((tm, gates.shape[1]), row),
                  pl.BlockSpec((tm, d), row),
                  full(w_s), full(w_l), full(w_m), full(w_o),
                  pl.BlockSpec((1, d), const)],
        out_specs=pl.BlockSpec((tm, d), row),
        out_shape=jax.ShapeDtypeStruct((t, d), F32),
        scratch_shapes=[pltpu.VMEM((tm, d), BF16)],
        compiler_params=_cparams(("parallel",)),
        name="merge",
    )(y_ssd, y_lru, q, kv, gates, x2d, w_s, w_l, w_m, w_o, final_g.reshape(1, d))


def kernel(x, mem, norm_g, w_in, ssd_conv_w, ssd_conv_b, ssd_dt_bias, ssd_a_log, ssd_d, ssd_norm_g, lru_conv_w, lru_conv_b, lru_w_a, lru_b_a, lru_w_x, lru_b_x, lru_lambda, mem_norm_g, w_kv, w_br_ssd, w_br_lru, w_br_mem, w_out, final_g):
    bsz, seq, d = x.shape
    depth = norm_g.shape[0]
    t = bsz * seq
    heads = ssd_dt_bias.shape[1]
    ssd_w = heads * SSD_HEAD_DIM
    conv_ch = ssd_conv_w.shape[2]
    lru_w = lru_conv_w.shape[2]
    x2d = x.reshape(t, d)
    mem2d = mem.reshape(bsz * mem.shape[1], d)
    front = ssd_w + conv_ch
    tn = lambda n: _pick(n, (1024, 768, 512, 256, 128))
    for l in range(depth):
        wt = jnp.swapaxes(w_in[l], 0, 1)
        rest = front + heads

        h, q, dt_raw = _norm_q_dt(x2d, norm_g[l], wt, rest + 2 * lru_w, d, front, seq)
        zs = _proj(h, wt, 0, ssd_w, tn(ssd_w), seq, "proj_z", act="silu")
        xbc = _proj(h, wt, ssd_w, conv_ch, tn(conv_ch), seq, "proj_xbc", act="silu",
                    conv_w=ssd_conv_w[l], conv_b=ssd_conv_b[l])
        slg = _proj(h, wt, rest, lru_w, tn(lru_w), seq, "proj_lg", act="silu")
        xl = _proj(h, wt, rest + lru_w, lru_w, tn(lru_w), seq, "proj_lx",
                   conv_w=lru_conv_w[l], conv_b=lru_conv_b[l])
        gates = _proj(h, wt, rest + 2 * lru_w + d, 3 * d, tn(3 * d), seq, "proj_gates", act="sigmoid")

        y_ssd = _ssd_branch(zs, xbc, dt_raw, ssd_dt_bias[l], ssd_a_log[l], ssd_d[l],
                            ssd_norm_g[l].reshape(-1), bsz, seq)
        y_lru = _lru_branch(slg, xl, lru_w_a[l], lru_b_a[l].reshape(-1), lru_w_x[l], lru_b_x[l].reshape(-1),
                            lru_lambda[l], bsz, seq)
        kv = _mem_kv(mem2d, mem_norm_g[l], w_kv[l].astype(BF16), _pick(2 * d, (512, 256, 128)))
        x2d = _merge(y_ssd, y_lru, q, kv, gates, x2d, w_br_ssd[l].astype(BF16), w_br_lru[l].astype(BF16),
                     w_br_mem[l].astype(BF16), w_out[l].astype(BF16), final_g, l == depth - 1, seq)
    return x2d.reshape(bsz, seq, d)
```

```python
import functools

import jax
import jax.numpy as jnp
from jax import lax
from jax.experimental import pallas as pl
from jax.experimental.pallas import tpu as pltpu

F32 = jnp.float32
BF16 = jnp.bfloat16

EPS = 1e-6
CONV_WIDTH = 4
SSD_HEAD_DIM = 64
SSD_GROUPS = 4
SSD_STATE = 128
SSD_CHUNK = 128
LRU_C = 8.0
MEM_HEADS = 4
LANES = 128
SUBLANES = 8
LRU_TILE_GROUPS = 4
SCAN_STRIDE = 4
SCAN_ROWS = SCAN_STRIDE * SUBLANES

VMEM_LIMIT = 56 * 1024 * 1024


def _cparams(sem):
    return pltpu.CompilerParams(dimension_semantics=sem, vmem_limit_bytes=VMEM_LIMIT)


LOG2E = 1.4426950408889634
TINY = 1e-30


def _sigmoid(x):
    return 1.0 / (1.0 + jnp.exp2(x * (-LOG2E)))


def _sqrt_nonneg(x):
    return x * lax.rsqrt(jnp.maximum(x, TINY))


def _silu(x):
    return x * _sigmoid(x)


def _softplus(x):
    return jnp.maximum(x, 0.0) + jnp.log1p(jnp.exp(-jnp.abs(x)))


def _split2(x):
    hi = x.astype(BF16)
    lo = (x - hi.astype(F32)).astype(BF16)
    return hi, lo


def _split3(x):
    hi = x.astype(BF16)
    r = x - hi.astype(F32)
    mid = r.astype(BF16)
    lo = (r - mid.astype(F32)).astype(BF16)
    return hi, mid, lo


def _dot(a, b):
    return jnp.dot(a, b, preferred_element_type=F32)


def _dot_nt(a, b):
    return lax.dot_general(a, b, (((1,), (1,)), ((), ())), preferred_element_type=F32)


def _pick(n, prefs):
    for p in prefs:
        if n % p == 0:
            return p
    return n


def _proj_kernel(*refs, sub, act, conv, blocks_per_seq):
    if conv:
        h_ref, wt_ref, cw_ref, cb_ref, o_ref, w_ref, pad_ref = refs
    else:
        h_ref, wt_ref, o_ref, w_ref, acc0_ref, acc1_ref = refs
        acc_refs = (acc0_ref, acc1_ref)
    tm = h_ref.shape[0]
    nsub = tm // sub
    i = pl.program_id(1)

    @pl.when(i == 0)
    def _():
        w_ref[...] = wt_ref[...].T.astype(BF16)

    if conv:
        @pl.when(i % blocks_per_seq == 0)
        def _():
            pad_ref[:, 0:2 * SUBLANES, :] = jnp.zeros((pad_ref.shape[0], 2 * SUBLANES, LANES), F32)

        for r in range(nsub):
            rows = slice(r * sub, (r + 1) * sub)
            acc = _dot(h_ref[rows, :], w_ref[...])
            for c in range(pad_ref.shape[0]):
                cols = slice(c * LANES, (c + 1) * LANES)
                x = acc[:, cols]
                pad_ref[c, pl.ds(2 * SUBLANES, sub, stride=2), :] = x
                y = cb_ref[:, cols] + cw_ref[CONV_WIDTH - 1:CONV_WIDTH, cols] * x
                for s in range(1, CONV_WIDTH):
                    y = y + (cw_ref[CONV_WIDTH - 1 - s:CONV_WIDTH - s, cols]
                             * pad_ref[c, pl.ds(2 * (SUBLANES - s), sub, stride=2), :])
                pad_ref[c, pl.ds(0, SUBLANES, stride=2), :] = x[sub - SUBLANES:, :]
                o_ref[rows, cols] = (_silu(y) if act == "silu" else y).astype(o_ref.dtype)
        return

    for r in range(nsub + 1):
        if r > 0:
            acc = acc_refs[(r - 1) % 2][...]
            o_ref[(r - 1) * sub:r * sub, :] = (_silu(acc) if act == "silu" else _sigmoid(acc)).astype(o_ref.dtype)
        if r < nsub:
            acc_refs[r % 2][...] = _dot(h_ref[r * sub:(r + 1) * sub, :], w_ref[...])


def _proj(h, wt, row0, ncols, tn, seq, name, act=None, conv_w=None, conv_b=None, out_dtype=BF16):
    t, k = h.shape
    conv = conv_w is not None
    tall = not conv and tn >= 1024
    tm = _pick(seq, (4096, 2048, 1024, 512, 256, 128)) if tall else _pick(seq, (2048, 1024, 512, 256, 128))
    sub = min(tm, 2048 if conv else 256)
    in_specs = [pl.BlockSpec((tm, k), lambda j, i: (i, 0)),
                pl.BlockSpec((pl.Element(tn), pl.Element(k)),
                             lambda j, i: (pl.multiple_of(row0 + j * tn, SUBLANES), 0))]
    args = [h, wt]
    scratch = [pltpu.VMEM((k, tn), BF16)]
    if conv:
        in_specs += [pl.BlockSpec((CONV_WIDTH, tn), lambda j, i: (0, j)),
                     pl.BlockSpec((1, tn), lambda j, i: (0, j))]
        args += [conv_w, conv_b.reshape(1, ncols)]
        scratch += [pltpu.VMEM((tn // LANES, 2 * (sub + SUBLANES), LANES), F32)]
    else:
        scratch += [pltpu.VMEM((sub, tn), F32), pltpu.VMEM((sub, tn), F32)]
    return pl.pallas_call(
        functools.partial(_proj_kernel, sub=sub, act=act, conv=conv, blocks_per_seq=seq // tm),
        grid=(ncols // tn, t // tm),
        in_specs=in_specs,
        out_specs=pl.BlockSpec((tm, tn), lambda j, i: (i, j)),
        out_shape=jax.ShapeDtypeStruct((t, ncols), out_dtype),
        scratch_shapes=scratch,
        compiler_params=_cparams(("arbitrary", "arbitrary")),
        name=name,
    )(*args)


def _norm_q_dt_kernel(x_ref, g_ref, wqt_ref, wdtt_ref, h_ref, q_ref, dt_ref, wq_ref, wdt_ref, *, sub):
    tm = x_ref.shape[0]

    @pl.when(pl.program_id(0) == 0)
    def _():
        wq_ref[...] = wqt_ref[...].T.astype(BF16)
        wdt_ref[...] = wdtt_ref[...].T.astype(BF16)

    for r in range(tm // sub):
        rows = slice(r * sub, (r + 1) * sub)
        x = x_ref[rows, :]
        ms = jnp.mean(x * x, axis=-1, keepdims=True)
        h = (x * lax.rsqrt(ms + EPS) * g_ref[...]).astype(BF16)
        h_ref[rows, :] = h
        q_ref[rows, :] = _dot(h, wq_ref[...]).astype(q_ref.dtype)
        dt_ref[rows, :] = _dot(h, wdt_ref[...])


def _norm_q_dt(x2d, g, wt, row0, ncols, dt_row0, seq):
    t, k = x2d.shape
    tm = _pick(seq, (1024, 512, 256, 128))
    sub = min(tm, 256)
    return pl.pallas_call(
        functools.partial(_norm_q_dt_kernel, sub=sub),
        grid=(t // tm,),
        in_specs=[pl.BlockSpec((tm, k), lambda i: (i, 0)),
                  pl.BlockSpec((1, k), lambda i: (0, 0)),
                  pl.BlockSpec((pl.Element(ncols), pl.Element(k)), lambda i: (row0, 0)),
                  pl.BlockSpec((pl.Element(LANES), pl.Element(k)), lambda i: (dt_row0, 0))],
        out_specs=[pl.BlockSpec((tm, k), lambda i: (i, 0)),
                   pl.BlockSpec((tm, ncols), lambda i: (i, 0)),
                   pl.BlockSpec((tm, LANES), lambda i: (i, 0))],
        out_shape=[jax.ShapeDtypeStruct((t, k), BF16),
                   jax.ShapeDtypeStruct((t, ncols), BF16),
                   jax.ShapeDtypeStruct((t, LANES), F32)],
        scratch_shapes=[pltpu.VMEM((k, ncols), BF16), pltpu.VMEM((k, LANES), BF16)],
        compiler_params=_cparams(("arbitrary",)),
        name="norm_q_dt",
    )(x2d, g.reshape(1, k), wt, wt)


def _ssd_kernel(zs_ref, xbc_ref, dt_ref, dtb_ref, alog_ref, dexp_ref, ng_ref, e_ref,
                o_ref, state_ref, yacc_ref, rowterm_ref, wgt_ref, acs_ref, eahi_ref, ealo_ref,
                *, width, groups, nstate, heads):
    c = pl.program_id(1)
    L = SSD_CHUNK
    gw = width // groups
    pairs = gw // LANES
    b_off = width
    c_off = width + groups * nstate

    @pl.when(c == 0)
    def _():
        state_ref[...] = jnp.zeros(state_ref.shape, F32)

    row = lax.broadcasted_iota(jnp.int32, (L, L), 0)
    col = lax.broadcasted_iota(jnp.int32, (L, L), 1)
    causal = row >= col
    triu = (row <= col).astype(BF16)
    lo_half = lax.broadcasted_iota(jnp.int32, (L, LANES), 1) < SSD_HEAD_DIM

    nchunks = zs_ref.shape[0] // L
    a_neg = -jnp.exp(alog_ref[...])
    dt_ts = [_softplus(dt_ref[k * L:(k + 1) * L, :].T[:heads, :] + dtb_ref[...]) for k in range(nchunks)]
    parts = [_split3(dt_t * a_neg) for dt_t in dt_ts]
    acs_ts = [sum(_dot(part, triu) for part in p) for p in parts]
    for k in range(nchunks):
        acs_t, dt_t = acs_ts[k], dt_ts[k]
        rowterm_ref[k] = (acs_t - jnp.log(dt_t)) * LOG2E
        wgt_ref[k] = dt_t * jnp.exp(acs_t[:, L - 1:L] - acs_t)
        acs_n = jnp.concatenate([acs_t * LOG2E, jnp.zeros((LANES - heads, L), F32)], axis=0).T
        acs_ref[k] = acs_n
        ea_hi, ea_lo = _split2(jnp.exp2(acs_n))
        eahi_ref[k] = ea_hi
        ealo_ref[k] = ea_lo

    def chunk(ci, carry):
        rows = pl.ds(pl.multiple_of(ci * L, L), L)
        rowterm_t = rowterm_ref[ci]
        wgt_t = wgt_ref[ci]
        acs_n = acs_ref[ci]
        ea_hi = eahi_ref[ci]
        ea_lo = ealo_ref[ci]

        for g in range(groups):
            gcols = slice(g * gw, (g + 1) * gw)
            cg = xbc_ref[rows, c_off + g * nstate:c_off + (g + 1) * nstate]
            bg = xbc_ref[rows, b_off + g * nstate:b_off + (g + 1) * nstate]
            cb = _dot_nt(cg, bg)
            bgt = bg.astype(F32).T
            ea_x = _dot(ea_hi, e_ref[:, gcols]) + _dot(ea_lo, e_ref[:, gcols])
            yacc_ref[g] = _dot(cg, state_ref[g].astype(BF16)) * ea_x
            ssq = jnp.zeros((L, LANES), F32)
            for j in range(pairs):
                h0 = (g * pairs + j) * 2
                tile = slice(j * LANES, (j + 1) * LANES)
                gtile = slice(g * gw + j * LANES, g * gw + (j + 1) * LANES)
                top, bot = [], []
                for h in (h0, h0 + 1):
                    seg = acs_n[:, h:h + 1] - rowterm_t[h:h + 1, :]
                    top.append((jnp.exp2(jnp.where(causal, seg, -jnp.inf)) * cb).astype(BF16))
                    bot.append((bgt * wgt_t[h:h + 1, :]).astype(BF16))
                lhs = jnp.concatenate([jnp.concatenate(top, axis=1), jnp.concatenate(bot, axis=1)],
                                      axis=0)
                xp = xbc_ref[rows, gtile]
                zero = jnp.zeros_like(xp)
                rhs = jnp.concatenate([jnp.where(lo_half, xp, zero), jnp.where(lo_half, zero, xp)], axis=0)
                res = _dot(lhs, rhs)
                state_ref[g, :, tile] = state_ref[g, :, tile] * ea_x[L - 1:L, tile] + res[L:, :]
                y = res[:L, :] + yacc_ref[g, :, tile] + xp.astype(F32) * dexp_ref[:, gtile]
                y = y * zs_ref[rows, gtile].astype(F32)
                yacc_ref[g, :, tile] = y
                ssq = ssq + y * y
            scale = lax.rsqrt(jnp.sum(ssq, axis=-1, keepdims=True) * (1.0 / gw) + EPS)
            o_ref[rows, gcols] = (yacc_ref[g] * scale * ng_ref[:, gcols]).astype(o_ref.dtype)
        return carry

    lax.fori_loop(0, zs_ref.shape[0] // L, chunk, 0, unroll=4)


def _ssd_branch(zs, xbc, dt_raw, dt_bias, a_log, d_skip, norm_g, bsz, seq):
    t, width = zs.shape
    conv_ch = xbc.shape[1]
    heads = dt_bias.shape[0]
    groups = SSD_GROUPS
    nstate = SSD_STATE
    L = SSD_CHUNK
    rows = _pick(seq, (8 * L, 4 * L, 2 * L, L))
    nb = seq // rows
    dtb = jnp.broadcast_to(dt_bias[:, None], (heads, L))
    alog = jnp.broadcast_to(a_log[:, None], (heads, L))
    dexp = jnp.repeat(d_skip, SSD_HEAD_DIM).reshape(1, width)
    expand = (jnp.arange(LANES)[:, None] == (jnp.arange(width) // SSD_HEAD_DIM)[None, :]).astype(BF16)
    row = lambda b, c: (b * nb + c, 0)
    const = lambda b, c: (0, 0)
    return pl.pallas_call(
        functools.partial(_ssd_kernel, width=width, groups=groups, nstate=nstate, heads=heads),
        grid=(bsz, nb),
        in_specs=[pl.BlockSpec((rows, width), row),
                  pl.BlockSpec((rows, conv_ch), row),
                  pl.BlockSpec((rows, LANES), row),
                  pl.BlockSpec((heads, L), const),
                  pl.BlockSpec((heads, L), const),
                  pl.BlockSpec((1, width), const),
                  pl.BlockSpec((1, width), const),
                  pl.BlockSpec((LANES, width), const)],
        out_specs=pl.BlockSpec((rows, width), row),
        out_shape=jax.ShapeDtypeStruct((t, width), BF16),
        scratch_shapes=[pltpu.VMEM((groups, nstate, width // groups), F32),
                        pltpu.VMEM((groups, L, width // groups), F32),
                        pltpu.VMEM((rows // L, heads, L), F32),
                        pltpu.VMEM((rows // L, heads, L), F32),
                        pltpu.VMEM((rows // L, L, LANES), F32),
                        pltpu.VMEM((rows // L, L, LANES), BF16),
                        pltpu.VMEM((rows // L, L, LANES), BF16)],
        compiler_params=_cparams(("arbitrary", "arbitrary")),
        name="ssd",
    )(zs, xbc, dt_raw, dtb, alog, dexp, norm_g.reshape(1, width), expand)


def _lru_kernel(slg_ref, xl_ref, wg_ref, ba_ref, bx_ref, lam_ref, o_ref, a_ref, u_ref, carry_ref, *, rt):
    c = pl.program_id(1)
    rows, width = xl_ref.shape
    nt = width // LANES
    tg = width // LRU_TILE_GROUPS
    tiles_per_group = tg // LANES

    @pl.when(c == 0)
    def _():
        carry_ref[...] = jnp.zeros(carry_ref.shape, F32)

    def gate_tile(ti, carry):
        rs = pl.ds(pl.multiple_of(ti * rt, rt), rt)
        for g in range(LRU_TILE_GROUPS):
            gs = slice(g * tg, (g + 1) * tg)
            xl = xl_ref[rs, gs]
            gg = _dot(xl, wg_ref[g])
            tr = jnp.tanh(gg[:, :tg] + ba_ref[:, gs])
            ti = jnp.tanh(gg[:, tg:] + bx_ref[:, gs])
            ch = (-0.5 * LRU_C * LOG2E) * _softplus(-lam_ref[:, gs])
            a = jnp.exp2(tr * ch + ch)
            u = _sqrt_nonneg(1.0 - a * a) * ((0.5 * ti + 0.5) * xl.astype(F32))
            for k in range(tiles_per_group):
                a_ref[g * tiles_per_group + k, rs, :] = a[:, k * LANES:(k + 1) * LANES]
                u_ref[g * tiles_per_group + k, rs, :] = u[:, k * LANES:(k + 1) * LANES]
        return carry

    lax.fori_loop(0, rows // rt, gate_tile, 0, unroll=4)

    sub = lax.broadcasted_iota(jnp.int32, (SUBLANES, LANES), 0)

    def mini(m, carry):
        base = pl.multiple_of(m * SCAN_ROWS, SCAN_ROWS)
        for t in range(nt):
            cprev = carry_ref[:, t * LANES:(t + 1) * LANES]
            av = [a_ref[t, pl.ds(base + j, SUBLANES, stride=SCAN_STRIDE), :] for j in range(SCAN_STRIDE)]
            uv = [u_ref[t, pl.ds(base + j, SUBLANES, stride=SCAN_STRIDE), :] for j in range(SCAN_STRIDE)]
            h, p = uv[0], av[0]
            for j in range(1, SCAN_STRIDE):
                h = av[j] * h + uv[j]
                p = p * av[j]
            d = 1
            while d < SUBLANES:
                hs = pltpu.roll(h, d, 0)
                ps = pltpu.roll(p, d, 0)
                keep = sub < d
                h = jnp.where(keep, h, p * hs + h)
                p = jnp.where(keep, p, p * ps)
                d *= 2
            e = p * cprev + h
            hh = jnp.where(sub == 0, cprev, pltpu.roll(e, 1, 0))
            for j in range(SCAN_STRIDE):
                hh = av[j] * hh + uv[j]
                u_ref[t, pl.ds(base + j, SUBLANES, stride=SCAN_STRIDE), :] = hh
            carry_ref[:, t * LANES:(t + 1) * LANES] = jnp.broadcast_to(e[SUBLANES - 1:SUBLANES, :],
                                                                       (SUBLANES, LANES))
        return carry

    lax.fori_loop(0, rows // SCAN_ROWS, mini, 0, unroll=4)

    for t in range(nt):
        ts = slice(t * LANES, (t + 1) * LANES)
        o_ref[:, ts] = (u_ref[t] * slg_ref[:, ts].astype(F32)).astype(o_ref.dtype)


def _lru_branch(slg, xl, w_a, b_a, w_x, b_x, lam, bsz, seq):
    t, width = xl.shape
    rows = _pick(seq, (1024, 512, 256, 128))
    nb = seq // rows
    nblk, blk = w_a.shape[0], w_a.shape[1]
    per = nblk // LRU_TILE_GROUPS
    tg = width // LRU_TILE_GROUPS

    def tile_diag(w):
        wt = w.reshape(LRU_TILE_GROUPS, per, blk, blk)
        eye = jnp.eye(per, dtype=w.dtype)
        return jnp.einsum("gpij,pq->gpiqj", wt, eye).reshape(LRU_TILE_GROUPS, tg, tg)

    wg = (0.5 * jnp.concatenate([tile_diag(w_a), tile_diag(w_x)], axis=2)).astype(BF16)
    b_a, b_x = 0.5 * b_a, 0.5 * b_x
    row = lambda b, c: (b * nb + c, 0)
    const = lambda b, c: (0, 0)
    return pl.pallas_call(
        functools.partial(_lru_kernel, rt=min(rows, 128)),
        grid=(bsz, nb),
        in_specs=[pl.BlockSpec((rows, width), row),
                  pl.BlockSpec((rows, width), row),
                  pl.BlockSpec((LRU_TILE_GROUPS, tg, 2 * tg), lambda b, c: (0, 0, 0)),
                  pl.BlockSpec((1, width), const),
                  pl.BlockSpec((1, width), const),
                  pl.BlockSpec((1, width), const)],
        out_specs=pl.BlockSpec((rows, width), row),
        out_shape=jax.ShapeDtypeStruct((t, width), BF16),
        scratch_shapes=[pltpu.VMEM((width // LANES, rows, LANES), F32),
                        pltpu.VMEM((width // LANES, rows, LANES), F32),
                        pltpu.VMEM((SUBLANES, width), F32)],
        compiler_params=_cparams(("arbitrary", "arbitrary")),
        name="lru",
    )(slg, xl, wg, b_a.reshape(1, width), b_x.reshape(1, width), lam.reshape(1, width))


def _kv_kernel(m_ref, g_ref, w_ref, o_ref):
    m = m_ref[...]
    ms = jnp.mean(m * m, axis=-1, keepdims=True)
    mn = (m * lax.rsqrt(ms + EPS) * g_ref[...]).astype(BF16)
    o_ref[...] = _dot(mn, w_ref[...]).astype(o_ref.dtype)


def _mem_kv(mem2d, g, w_kv_b, tn):
    m, d = mem2d.shape
    n = w_kv_b.shape[1]
    return pl.pallas_call(
        _kv_kernel,
        grid=(n // tn,),
        in_specs=[pl.BlockSpec((m, d), lambda j: (0, 0)),
                  pl.BlockSpec((1, d), lambda j: (0, 0)),
                  pl.BlockSpec((d, tn), lambda j: (0, j))],
        out_specs=pl.BlockSpec((m, tn), lambda j: (0, j)),
        out_shape=jax.ShapeDtypeStruct((m, n), BF16),
        compiler_params=_cparams(("parallel",)),
        name="mem_kv",
    )(mem2d, g.reshape(1, d), w_kv_b)


def _merge_kernel(ys_ref, yl_ref, q_ref, kv_ref, gt_ref, x_ref, ws_ref, wl_ref, wm_ref, wo_ref, fg_ref, o_ref,
                  ym_ref, *, final_norm):
    d = x_ref.shape[1]
    hd = d // MEM_HEADS
    scale = hd ** -0.5
    sub = min(q_ref.shape[0], 256)
    for bi in range(q_ref.shape[0] // sub):
        rows = slice(bi * sub, (bi + 1) * sub)
        for h in range(MEM_HEADS):
            hs = slice(h * hd, (h + 1) * hd)
            s = _dot_nt(q_ref[rows, hs], kv_ref[:, hs]) * scale
            p = jnp.exp(s - jnp.max(s, axis=-1, keepdims=True))
            l = jnp.sum(p, axis=-1, keepdims=True)
            o = _dot(p.astype(BF16), kv_ref[:, d + h * hd:d + (h + 1) * hd]) / l
            ym_ref[rows, hs] = o.astype(ym_ref.dtype)
    merged = (gt_ref[:, :d].astype(F32) * _dot(ys_ref[...], ws_ref[...])
              + gt_ref[:, d:2 * d].astype(F32) * _dot(yl_ref[...], wl_ref[...])
              + gt_ref[:, 2 * d:].astype(F32) * _dot(ym_ref[...], wm_ref[...]))
    xn = x_ref[...] + _dot(merged.astype(BF16), wo_ref[...])
    if final_norm:
        ms = jnp.mean(xn * xn, axis=-1, keepdims=True)
        xn = xn * lax.rsqrt(ms + EPS) * fg_ref[...]
    o_ref[...] = xn


def _merge(y_ssd, y_lru, q, kv, gates, x2d, w_s, w_l, w_m, w_o, final_g, final_norm, seq):
    t, d = x2d.shape
    tm = _pick(seq, (512, 256, 128))
    blocks_per_seq = seq // tm
    m = kv.shape[0] // (t // seq)
    row = lambda i: (i, 0)
    const = lambda i: (0, 0)
    full = lambda a: pl.BlockSpec(a.shape, const)
    return pl.pallas_call(
        functools.partial(_merge_kernel, final_norm=final_norm),
        grid=(t // tm,),
        in_specs=[pl.BlockSpec((tm, y_ssd.shape[1]), row),
                  pl.BlockSpec((tm, y_lru.shape[1]), row),
                  pl.BlockSpec((tm, d), row),
                  pl.BlockSpec((m, 2 * d), lambda i: (i // blocks_per_seq, 0)),
                  pl.BlockSpec((tm, gates.shape[1]), row),
                  pl.BlockSpec((tm, d), row),
                  full(w_s), full(w_l), full(w_m), full(w_o),
                  pl.BlockSpec((1, d), const)],
        out_specs=pl.BlockSpec((tm, d), row),
        out_shape=jax.ShapeDtypeStruct((t, d), F32),
        scratch_shapes=[pltpu.VMEM((tm, d), BF16)],
        compiler_params=_cparams(("parallel",)),
        name="merge",
    )(y_ssd, y_lru, q, kv, gates, x2d, w_s, w_l, w_m, w_o, final_g.reshape(1, d))


def kernel(x, mem, norm_g, w_in, ssd_conv_w, ssd_conv_b, ssd_dt_bias, ssd_a_log, ssd_d, ssd_norm_g, lru_conv_w, lru_conv_b, lru_w_a, lru_b_a, lru_w_x, lru_b_x, lru_lambda, mem_norm_g, w_kv, w_br_ssd, w_br_lru, w_br_mem, w_out, final_g):
    bsz, seq, d = x.shape
    depth = norm_g.shape[0]
    t = bsz * seq
    heads = ssd_dt_bias.shape[1]
    ssd_w = heads * SSD_HEAD_DIM
    conv_ch = ssd_conv_w.shape[2]
    lru_w = lru_conv_w.shape[2]
    x2d = x.reshape(t, d)
    mem2d = mem.reshape(bsz * mem.shape[1], d)
    front = ssd_w + conv_ch
    tn = lambda n: _pick(n, (1024, 768, 512, 256, 128))
    for l in range(depth):
        wt = jnp.swapaxes(w_in[l], 0, 1)
        rest = front + heads

        h, q, dt_raw = _norm_q_dt(x2d, norm_g[l], wt, rest + 2 * lru_w, d, front, seq)
        zs = _proj(h, wt, 0, ssd_w, tn(ssd_w), seq, "proj_z", act="silu")
        xbc = _proj(h, wt, ssd_w, conv_ch, tn(conv_ch), seq, "proj_xbc", act="silu",
                    conv_w=ssd_conv_w[l], conv_b=ssd_conv_b[l])
        slg = _proj(h, wt, rest, lru_w, tn(lru_w), seq, "proj_lg", act="silu")
        xl = _proj(h, wt, rest + lru_w, lru_w, tn(lru_w), seq, "proj_lx",
                   conv_w=lru_conv_w[l], conv_b=lru_conv_b[l])
        gates = _proj(h, wt, rest + 2 * lru_w + d, 3 * d, tn(3 * d), seq, "proj_gates", act="sigmoid")

        y_ssd = _ssd_branch(zs, xbc, dt_raw, ssd_dt_bias[l], ssd_a_log[l], ssd_d[l],
                            ssd_norm_g[l].reshape(-1), bsz, seq)
        y_lru = _lru_branch(slg, xl, lru_w_a[l], lru_b_a[l].reshape(-1), lru_w_x[l], lru_b_x[l].reshape(-1),
                            lru_lambda[l], bsz, seq)
        kv = _mem_kv(mem2d, mem_norm_g[l], w_kv[l].astype(BF16), _pick(2 * d, (512, 256, 128)))
        x2d = _merge(y_ssd, y_lru, q, kv, gates, x2d, w_br_ssd[l].astype(BF16), w_br_lru[l].astype(BF16),
                     w_br_mem[l].astype(BF16), w_out[l].astype(BF16), final_g, l == depth - 1, seq)
    return x2d.reshape(bsz, seq, d)
```
